```python
import jax, jax.numpy as jnp
from jax import lax
import numpy as np

D_MODEL = 1024
BATCH = 4
SEQ = 4096
DEPTH = 4
DEC_BATCH = 128
DEC_SEQ = 4
PAST_LEN = 8192
PAGE_SIZE = 128

N_MIXERS = 3
HEAD_DIM = 64
D_FF = 2816
CHUNK = 128
D_SGU = 2 * D_MODEL
SGU_GROUPS = 16
SGU_GROUP_DIM = D_SGU // SGU_GROUPS
FOX_HEADS = D_MODEL // HEAD_DIM
SWA_Q_HEADS = D_MODEL // HEAD_DIM
SWA_KV_HEADS = SWA_Q_HEADS // 4
SWA_GROUP = SWA_Q_HEADS // SWA_KV_HEADS
WINDOW = 128
Q_BLOCK = 128
ROPE_THETA = 10000.0
NORM_EPS = 1e-6
RES_HALF = 0.5
FORGET_BIAS = 8.0
NEG_INF = -1e30
N_LAYERS_A = (DEPTH + 2) // 3
N_LAYERS_B = (DEPTH + 1) // 3
N_LAYERS_C = DEPTH // 3

kernel_name = 'hybrid_gmlp_fox_swa_macaron_step'


def rms_norm(x, g):
    xf = x.astype(jnp.float32)
    y = xf * lax.rsqrt(jnp.mean(xf * xf, axis=-1, keepdims=True) + NORM_EPS) * g.astype(jnp.float32)
    return y.astype(x.dtype)


def layer_norm(x, g, b):
    xf = x.astype(jnp.float32)
    mu = jnp.mean(xf, axis=-1, keepdims=True)
    var = jnp.mean(jnp.square(xf - mu), axis=-1, keepdims=True)
    y = (xf - mu) * lax.rsqrt(var + NORM_EPS) * g.astype(jnp.float32) + b.astype(jnp.float32)
    return y.astype(x.dtype)


def swiglu(x, w_in, w_out):
    h = x @ w_in
    return (jax.nn.silu(h[..., :D_FF]) * h[..., D_FF:]) @ w_out


def half_ffn(x, g_pre, g_post, w_in, w_out):
    return x + RES_HALF * rms_norm(swiglu(rms_norm(x, g_pre), w_in, w_out), g_post)


def rope(x, pos):
    half = HEAD_DIM // 2
    inv = ROPE_THETA ** (-jnp.arange(half, dtype=jnp.float32) * 2.0 / HEAD_DIM)
    ang = pos.astype(jnp.float32)[:, None] * inv[None, :]
    cos = jnp.cos(ang)[None, :, None, :]
    sin = jnp.sin(ang)[None, :, None, :]
    xf = x.astype(jnp.float32)
    x1, x2 = xf[..., :half], xf[..., half:]
    return jnp.concatenate([x1 * cos - x2 * sin, x2 * cos + x1 * sin], axis=-1).astype(x.dtype)


def sink_softmax(s, sink):
    m = jnp.maximum(jnp.max(s, axis=-1, keepdims=True), sink)
    e = jnp.exp(s - m)
    return e / (jnp.sum(e, axis=-1, keepdims=True) + jnp.exp(sink - m))


def chunk_gmlp(h, w_in, ln_g, ln_b, w_s, b_s, w_out):
    B, S, _ = h.shape
    z = jax.nn.gelu(h @ w_in)
    u, v = z[..., :D_SGU], z[..., D_SGU:]
    v = layer_norm(v, ln_g, ln_b)
    pad = (-S) % CHUNK
    n_chunks = (S + pad) // CHUNK
    vp = jnp.pad(v, ((0, 0), (0, pad), (0, 0))).reshape(B, n_chunks, CHUNK, SGU_GROUPS, SGU_GROUP_DIM)
    tri = jnp.tril(jnp.ones((CHUNK, CHUNK), dtype=w_s.dtype))
    mix = jnp.einsum('gts,bnsgc->bntgc', w_s * tri, vp) + b_s.T[:, :, None]
    mix = mix.reshape(B, n_chunks * CHUNK, D_SGU)[:, :S]
    return (u * mix) @ w_out, v


def fox_project(h, w_in, b_f):
    B, S, _ = h.shape
    hd = FOX_HEADS * HEAD_DIM
    z = h @ w_in
    q = z[..., :hd].reshape(B, S, FOX_HEADS, HEAD_DIM)
    k = z[..., hd:2 * hd].reshape(B, S, FOX_HEADS, HEAD_DIM)
    v = z[..., 2 * hd:3 * hd].reshape(B, S, FOX_HEADS, HEAD_DIM)
    logf = jax.nn.log_sigmoid(z[..., 3 * hd:].astype(jnp.float32) + b_f.astype(jnp.float32))
    return q, k, v, logf


def fox_attend_prompt(q, k, v, logf):
    B, S, H, Dh = q.shape
    n_blk = S // Q_BLOCK
    scale = HEAD_DIM ** -0.5
    c = jnp.cumsum(logf, axis=1)
    c_keys = c.transpose(0, 2, 1)
    qb = q.reshape(B, n_blk, Q_BLOCK, H, Dh).swapaxes(0, 1)
    cb = c.reshape(B, n_blk, Q_BLOCK, H).swapaxes(0, 1)
    kpos = jnp.arange(S)

    def block(args):
        qi, ci, bi = args
        s = jnp.einsum('bqhd,bkhd->bhqk', qi, k).astype(jnp.float32) * scale
        s = s + ci.transpose(0, 2, 1)[..., None] - c_keys[:, :, None, :]
        qpos = bi * Q_BLOCK + jnp.arange(Q_BLOCK)
        s = jnp.where((kpos[None, :] <= qpos[:, None])[None, None], s, NEG_INF)
        p = jax.nn.softmax(s, axis=-1)
        return jnp.einsum('bhqk,bkhd->bqhd', p.astype(v.dtype), v)

    o = lax.map(block, (qb, cb, jnp.arange(n_blk)))
    return o.swapaxes(0, 1).reshape(B, S, H * Dh)


def fox_attend_sample(q, k_new, v_new, lf_new, k_pool, v_pool, lf_pool, page_table):
    Bd, T, H, Dh = q.shape
    P = page_table.shape[1] * PAGE_SIZE
    scale = HEAD_DIM ** -0.5
    kidx = jnp.arange(P + T)
    valid = (kidx[None, :] < P) | (kidx[None, :] - P <= jnp.arange(T)[:, None])

    def one(args):
        qi, kn, vn, lfn, pt = args
        kk = jnp.concatenate([k_pool[pt].reshape(P, H, Dh), kn], axis=0)
        vv = jnp.concatenate([v_pool[pt].reshape(P, H, Dh), vn], axis=0)
        lfp = lf_pool[pt].reshape(P, H).astype(jnp.float32)
        suffix = lax.cumsum(lfp, axis=0, reverse=True)
        past_bias = jnp.concatenate([suffix[1:], jnp.zeros((1, H), jnp.float32)], axis=0)
        cn = jnp.cumsum(lfn, axis=0)
        key_bias = jnp.concatenate([past_bias, -cn], axis=0)
        s = jnp.einsum('thd,khd->htk', qi, kk).astype(jnp.float32) * scale
        s = s + cn.T[:, :, None] + key_bias.T[:, None, :]
        s = jnp.where(valid[None], s, NEG_INF)
        p = jax.nn.softmax(s, axis=-1)
        return jnp.einsum('htk,khd->thd', p.astype(vv.dtype), vv)

    o = lax.map(one, (q, k_new, v_new, lf_new, page_table))
    return o.reshape(Bd, T, H * Dh)


def swa_project(h, w_in, pos):
    B, S, _ = h.shape
    qd = SWA_Q_HEADS * HEAD_DIM
    kd = SWA_KV_HEADS * HEAD_DIM
    z = h @ w_in
    q = rope(z[..., :qd].reshape(B, S, SWA_Q_HEADS, HEAD_DIM), pos)
    k = rope(z[..., qd:qd + kd].reshape(B, S, SWA_KV_HEADS, HEAD_DIM), pos)
    v = z[..., qd + kd:].reshape(B, S, SWA_KV_HEADS, HEAD_DIM)
    return q, k, v


def swa_attend_prompt(q, k, v, sinks):
    B, S = q.shape[:2]
    n = S // WINDOW
    scale = HEAD_DIM ** -0.5
    qb = q.reshape(B, n, WINDOW, SWA_KV_HEADS, SWA_GROUP, HEAD_DIM)
    kb = k.reshape(B, n, WINDOW, SWA_KV_HEADS, HEAD_DIM)
    vb = v.reshape(B, n, WINDOW, SWA_KV_HEADS, HEAD_DIM)
    kk = jnp.concatenate([jnp.concatenate([jnp.zeros_like(kb[:, :1]), kb[:, :-1]], axis=1), kb], axis=2)
    vv = jnp.concatenate([jnp.concatenate([jnp.zeros_like(vb[:, :1]), vb[:, :-1]], axis=1), vb], axis=2)
    s = jnp.einsum('bnqhgd,bnkhd->bnhgqk', qb, kk).astype(jnp.float32) * scale
    i = jnp.arange(WINDOW)[:, None]
    j = jnp.arange(2 * WINDOW)[None, :]
    diff = WINDOW + i - j
    rel = (diff >= 0) & (diff < WINDOW)
    key_abs = jnp.arange(n)[:, None, None] * WINDOW - WINDOW + j[None]
    valid = rel[None] & (key_abs >= 0)
    s = jnp.where(valid[None, :, None, None], s, NEG_INF)
    sink = sinks.astype(jnp.float32).reshape(1, 1, SWA_KV_HEADS, SWA_GROUP, 1, 1)
    p = sink_softmax(s, sink)
    o = jnp.einsum('bnhgqk,bnkhd->bnqhgd', p.astype(vv.dtype), vv)
    return o.reshape(B, S, SWA_Q_HEADS * HEAD_DIM)


def swa_attend_sample(q, k_new, v_new, k_buf, v_buf, sinks, past_len):
    Bd, T = q.shape[:2]
    scale = HEAD_DIM ** -0.5
    kk = jnp.concatenate([k_buf, k_new], axis=1)
    vv = jnp.concatenate([v_buf, v_new], axis=1)
    qg = q.reshape(Bd, T, SWA_KV_HEADS, SWA_GROUP, HEAD_DIM)
    s = jnp.einsum('bthgd,bkhd->bhgtk', qg, kk).astype(jnp.float32) * scale
    kpos = past_len - WINDOW + jnp.arange(WINDOW + T)
    qpos = past_len + jnp.arange(T)
    diff = qpos[:, None] - kpos[None, :]
    valid = (diff >= 0) & (diff < WINDOW)
    s = jnp.where(valid[None, None, None], s, NEG_INF)
    sink = sinks.astype(jnp.float32).reshape(1, SWA_KV_HEADS, SWA_GROUP, 1, 1)
    p = sink_softmax(s, sink)
    o = jnp.einsum('bhgtk,bkhd->bthgd', p.astype(vv.dtype), vv).reshape(Bd, T, SWA_Q_HEADS * HEAD_DIM)
    return o, kk[:, -WINDOW:], vv[:, -WINDOW:]


def setup_inputs(seed: int = 0) -> dict:
    key = jax.random.key(seed)
    ks = jax.random.split(key, 24)
    f32 = jnp.float32

    def nrm(k, shape, scale=1.0):
        return scale * jax.random.normal(k, shape, f32)

    n_pages = PAST_LEN // PAGE_SIZE
    n_used = DEC_BATCH * n_pages
    n_pool = n_used + max(1, n_used // 4)
    fox_w = FOX_HEADS * HEAD_DIM
    swa_w = (SWA_Q_HEADS + 2 * SWA_KV_HEADS) * HEAD_DIM
    swa_o = SWA_Q_HEADS * HEAD_DIM
    page_table = jax.random.permutation(ks[7], n_pool)[:n_used].reshape(DEC_BATCH, n_pages).astype(jnp.int32)
    return {
        'x_prompt': nrm(ks[0], (BATCH, SEQ, D_MODEL)),
        'x_sample': nrm(ks[1], (DEC_BATCH, DEC_SEQ, D_MODEL)),
        'cache_fox_k': nrm(ks[2], (N_LAYERS_B, n_pool, PAGE_SIZE, FOX_HEADS, HEAD_DIM)),
        'cache_fox_v': nrm(ks[3], (N_LAYERS_B, n_pool, PAGE_SIZE, FOX_HEADS, HEAD_DIM)),
        'cache_fox_logf': jax.nn.log_sigmoid(FORGET_BIAS + nrm(ks[4], (N_LAYERS_B, n_pool, PAGE_SIZE, FOX_HEADS))),
        'state_swa_k': nrm(ks[5], (N_LAYERS_C, DEC_BATCH, WINDOW, SWA_KV_HEADS, HEAD_DIM)),
        'state_swa_v': nrm(ks[6], (N_LAYERS_C, DEC_BATCH, WINDOW, SWA_KV_HEADS, HEAD_DIM)),
        'page_table': page_table,
        'norm_g': 1.0 + nrm(ks[8], (DEPTH, 6, D_MODEL), 0.05),
        'ffn_w_in': nrm(ks[9], (DEPTH, 2, D_MODEL, 2 * D_FF), D_MODEL ** -0.5),
        'ffn_w_out': nrm(ks[10], (DEPTH, 2, D_FF, D_MODEL), D_FF ** -0.5),
        'sgu_w_in': nrm(ks[11], (N_LAYERS_A, D_MODEL, 2 * D_SGU), D_MODEL ** -0.5),
        'sgu_ln_g': 1.0 + nrm(ks[12], (N_LAYERS_A, D_SGU), 0.05),
        'sgu_ln_b': nrm(ks[13], (N_LAYERS_A, D_SGU), 0.02),
        'sgu_w_s': nrm(ks[14], (N_LAYERS_A, SGU_GROUPS, CHUNK, CHUNK), CHUNK ** -0.5),
        'sgu_b_s': 1.0 + nrm(ks[15], (N_LAYERS_A, SGU_GROUPS, CHUNK), 0.1),
        'sgu_w_out': nrm(ks[16], (N_LAYERS_A, D_SGU, D_MODEL), D_SGU ** -0.5),
        'fox_w_in': nrm(ks[17], (N_LAYERS_B, D_MODEL, 3 * fox_w + FOX_HEADS), D_MODEL ** -0.5),
        'fox_b_f': FORGET_BIAS + nrm(ks[18], (N_LAYERS_B, FOX_HEADS), 0.1),
        'fox_w_out': nrm(ks[19], (N_LAYERS_B, fox_w, D_MODEL), fox_w ** -0.5),
        'swa_w_in': nrm(ks[20], (N_LAYERS_C, D_MODEL, swa_w), D_MODEL ** -0.5),
        'swa_sinks': nrm(ks[21], (N_LAYERS_C, SWA_Q_HEADS), 0.5),
        'swa_w_out': nrm(ks[22], (N_LAYERS_C, swa_o, D_MODEL), swa_o ** -0.5),
    }


def reference(x_prompt, x_sample, cache_fox_k, cache_fox_v, cache_fox_logf, state_swa_k, state_swa_v,
              page_table, norm_g, ffn_w_in, ffn_w_out, sgu_w_in, sgu_ln_g, sgu_ln_b, sgu_w_s, sgu_b_s,
              sgu_w_out, fox_w_in, fox_b_f, fox_w_out, swa_w_in, swa_sinks, swa_w_out):
    xp, xs = x_prompt, x_sample
    S = xp.shape[1]
    T = xs.shape[1]
    past_len = page_table.shape[1] * PAGE_SIZE
    pos_p = jnp.arange(S)
    pos_s = past_len + jnp.arange(T)
    fox_k_p, fox_v_p, fox_lf_p, fox_k_s, fox_v_s, fox_lf_s = [], [], [], [], [], []
    swa_k_p, swa_v_p, swa_k_s, swa_v_s = [], [], [], []
    sgu_v_s = []
    for layer in range(DEPTH):
        g = norm_g[layer]
        xp = half_ffn(xp, g[0], g[1], ffn_w_in[layer, 0], ffn_w_out[layer, 0])
        xs = half_ffn(xs, g[0], g[1], ffn_w_in[layer, 0], ffn_w_out[layer, 0])
        hp = rms_norm(xp, g[2])
        hs = rms_norm(xs, g[2])
        kind = layer % N_MIXERS
        j = layer // N_MIXERS
        if kind == 0:
            prm = (sgu_w_in[j], sgu_ln_g[j], sgu_ln_b[j], sgu_w_s[j], sgu_b_s[j], sgu_w_out[j])
            yp, _ = chunk_gmlp(hp, *prm)
            ys, v_rows = chunk_gmlp(hs, *prm)
            sgu_v_s.append(v_rows)
        elif kind == 1:
            q, k, v, lf = fox_project(hp, fox_w_in[j], fox_b_f[j])
            yp = fox_attend_prompt(q, k, v, lf) @ fox_w_out[j]
            fox_k_p.append(k)
            fox_v_p.append(v)
            fox_lf_p.append(lf)
            q, k, v, lf = fox_project(hs, fox_w_in[j], fox_b_f[j])
            ys = fox_attend_sample(q, k, v, lf, cache_fox_k[j], cache_fox_v[j], cache_fox_logf[j],
                                   page_table) @ fox_w_out[j]
            fox_k_s.append(k)
            fox_v_s.append(v)
            fox_lf_s.append(lf)
        else:
            q, k, v = swa_project(hp, swa_w_in[j], pos_p)
            yp = swa_attend_prompt(q, k, v, swa_sinks[j]) @ swa_w_out[j]
            swa_k_p.append(k[:, -WINDOW:])
            swa_v_p.append(v[:, -WINDOW:])
            q, k, v = swa_project(hs, swa_w_in[j], pos_s)
            o, kbuf, vbuf = swa_attend_sample(q, k, v, state_swa_k[j], state_swa_v[j], swa_sinks[j], past_len)
            ys = o @ swa_w_out[j]
            swa_k_s.append(kbuf)
            swa_v_s.append(vbuf)
        xp = xp + rms_norm(yp, g[3])
        xs = xs + rms_norm(ys, g[3])
        xp = half_ffn(xp, g[4], g[5], ffn_w_in[layer, 1], ffn_w_out[layer, 1])
        xs = half_ffn(xs, g[4], g[5], ffn_w_in[layer, 1], ffn_w_out[layer, 1])
    return (xp, xs,
            jnp.stack(fox_k_p), jnp.stack(fox_v_p), jnp.stack(fox_lf_p),
            jnp.stack(fox_k_s), jnp.stack(fox_v_s), jnp.stack(fox_lf_s),
            jnp.stack(swa_k_p), jnp.stack(swa_v_p), jnp.stack(swa_k_s), jnp.stack(swa_v_s),
            jnp.stack(sgu_v_s))
```

```python
import functools

import jax
import jax.numpy as jnp
from jax import lax
from jax.experimental import pallas as pl
from jax.experimental.pallas import tpu as pltpu

F32 = jnp.float32
BF16 = jnp.bfloat16

HEAD_DIM = 64
CHUNK = 128
SGU_GROUPS = 16
WINDOW = 128
PAGE_SIZE = 128
ROPE_THETA = 10000.0
NORM_EPS = 1e-6
RES_HALF = 0.5
NEG_INF = -1e30
SCALE = HEAD_DIM ** -0.5

LANES = 128
VMEM_LIMIT_BYTES = 56 * 1024 * 1024
ROW_TILE = 512
GMLP_ROW_TILE = 256
FLASH_TILE = 512
PAGES_PER_STEP = 8


def _params(n_axes):
    return pltpu.CompilerParams(dimension_semantics=("arbitrary",) * n_axes,
                                vmem_limit_bytes=VMEM_LIMIT_BYTES)


def _const_spec(shape):
    zeros = (0,) * len(shape)
    return pl.BlockSpec(shape, lambda *_: zeros, pipeline_mode=pl.Buffered(1))


def _row_spec(tm, width):
    return pl.BlockSpec((tm, width), lambda i: (i, 0))


def _rms(x, g):
    return x * lax.rsqrt(jnp.mean(x * x, axis=-1, keepdims=True) + NORM_EPS) * g


def _dot(a, b):
    return jnp.dot(a, b, preferred_element_type=F32)


def _dot_nt(a, b):
    return lax.dot_general(a, b, (((1,), (1,)), ((), ())), preferred_element_type=F32)


def _split2(x):
    hi = x.astype(BF16)
    lo = (x - hi.astype(F32)).astype(BF16)
    return hi, lo


def _ffn_body(x_ref, gpre_ref, gpost_ref, wi_ref, wo_ref, o_ref, *, d_ff, chunks):
    x = x_ref[...]
    h = _rms(x, gpre_ref[...]).astype(BF16)
    acc = None
    for c0, cw in chunks:
        gate = _dot(h, wi_ref[:, c0:c0 + cw])
        up = _dot(h, wi_ref[:, d_ff + c0:d_ff + c0 + cw])
        act = (gate * jax.nn.sigmoid(gate) * up).astype(BF16)
        y = _dot(act, wo_ref[c0:c0 + cw, :])
        acc = y if acc is None else acc + y
    o_ref[...] = x + RES_HALF * _rms(acc, gpost_ref[...])


def _ff_chunks(d_ff, width=1024):
    out, c0 = [], 0
    while c0 < d_ff:
        cw = min(width, d_ff - c0)
        out.append((c0, cw))
        c0 += cw
    return tuple(out)


def half_ffn(x, g_pre, g_post, w_in, w_out):
    n, d = x.shape
    d_ff = w_out.shape[0]
    tm = min(ROW_TILE, n)
    body = functools.partial(_ffn_body, d_ff=d_ff, chunks=_ff_chunks(d_ff))
    return pl.pallas_call(
        body,
        grid=(n // tm,),
        in_specs=[_row_spec(tm, d), _const_spec((1, d)), _const_spec((1, d)),
                  _const_spec(w_in.shape), _const_spec(w_out.shape)],
        out_specs=_row_spec(tm, d),
        out_shape=jax.ShapeDtypeStruct((n, d), F32),
        compiler_params=_params(1),
        name="half_ffn",
    )(x, g_pre, g_post, w_in, w_out)


def _gmlp_body(x_ref, g2_ref, g3_ref, win_ref, lng_ref, lnb_ref, ws_ref, bs_ref, wout_ref,
               *out_refs, d_sgu, tm, sample, emit_v):
    if emit_v:
        o_ref, v_ref, gated_ref = out_refs
    else:
        o_ref, gated_ref = out_refs
    x = x_ref[...]
    h = _rms(x, g2_ref[...]).astype(BF16)
    u = jax.nn.gelu(_dot(h, win_ref[:, :d_sgu]), approximate=True)
    zv = jax.nn.gelu(_dot(h, win_ref[:, d_sgu:]), approximate=True)
    mu = jnp.mean(zv, axis=-1, keepdims=True)
    var = jnp.mean(jnp.square(zv - mu), axis=-1, keepdims=True)
    v = (zv - mu) * lax.rsqrt(var + NORM_EPS) * lng_ref[...] + lnb_ref[...]
    if emit_v:
        v_ref[...] = v
    vb = v.astype(BF16)

    row = lax.broadcasted_iota(jnp.int32, (CHUNK, CHUNK), 0)
    col = lax.broadcasted_iota(jnp.int32, (CHUNK, CHUNK), 1)
    mask = col <= row
    if sample:
        mask = mask & ((row // 4) == (col // 4))
    gw = d_sgu // SGU_GROUPS
    for g in range(SGU_GROUPS):
        w = jnp.where(mask, ws_ref[g], 0.0).astype(BF16)
        b = bs_ref[:, g:g + 1]
        for c in range(tm // CHUNK):
            rows = slice(c * CHUNK, (c + 1) * CHUNK)
            cols = slice(g * gw, (g + 1) * gw)
            mix = _dot(w, vb[rows, cols]) + b
            gated_ref[rows, cols] = (u[rows, cols] * mix).astype(BF16)
    y = _dot(gated_ref[...], wout_ref[...])
    o_ref[...] = x + _rms(y, g3_ref[...])


def gmlp_mixer(x, g2, g3, w_in, ln_g, ln_b, w_s, b_s_t, w_out, *, sample):
    n, d = x.shape
    d_sgu = w_out.shape[0]
    tm = min(GMLP_ROW_TILE, n)
    body = functools.partial(_gmlp_body, d_sgu=d_sgu, tm=tm, sample=sample, emit_v=sample)
    out_shape = [jax.ShapeDtypeStruct((n, d), F32)]
    out_specs = [_row_spec(tm, d)]
    if sample:
        out_shape.append(jax.ShapeDtypeStruct((n, d_sgu), F32))
        out_specs.append(_row_spec(tm, d_sgu))
    res = pl.pallas_call(
        body,
        grid=(n // tm,),
        in_specs=[_row_spec(tm, d), _const_spec((1, d)), _const_spec((1, d)),
                  _const_spec(w_in.shape), _const_spec((1, d_sgu)), _const_spec((1, d_sgu)),
                  _const_spec(w_s.shape), _const_spec(b_s_t.shape), _const_spec(w_out.shape)],
        out_specs=out_specs,
        out_shape=out_shape,
        scratch_shapes=[pltpu.VMEM((tm, d_sgu), BF16)],
        compiler_params=_params(1),
        name="gmlp_sample" if sample else "gmlp_prompt",
    )(x, g2, g3, w_in, ln_g, ln_b, w_s, b_s_t, w_out)
    return res if sample else (res[0], None)


def _attn_out_body(x_ref, o_ref, w_ref, g_ref, y_ref):
    y = _dot(o_ref[...], w_ref[...])
    y_ref[...] = x_ref[...] + _rms(y, g_ref[...])


def attn_out(x, o, w_out, g_post):
    n, d = x.shape
    tm = min(ROW_TILE, n)
    return pl.pallas_call(
        _attn_out_body,
        grid=(n // tm,),
        in_specs=[_row_spec(tm, d), _row_spec(tm, o.shape[1]), _const_spec(w_out.shape),
                  _const_spec((1, d))],
        out_specs=_row_spec(tm, d),
        out_shape=jax.ShapeDtypeStruct((n, d), F32),
        compiler_params=_params(1),
        name="attn_out",
    )(x, o, w_out, g_post)


def _log_sigmoid(x):
    return jnp.minimum(x, 0.0) - jnp.log1p(jnp.exp(-jnp.abs(x)))


def _fox_proj_body(x_ref, g_ref, w_ref, bf_ref, tri_ref, q_ref, k_ref, v_ref, kb_ref, vb_ref,
                   lf_ref, c_ref, carry_ref, *, hd, n_heads, tiles_per_seq):
    x = x_ref[...]
    h = _rms(x, g_ref[...]).astype(BF16)
    z = _dot(h, w_ref[...])
    q_ref[...] = z[:, :hd].astype(BF16)
    k = z[:, hd:2 * hd]
    v = z[:, 2 * hd:3 * hd]
    k_ref[...] = k
    v_ref[...] = v
    kb_ref[...] = k.astype(BF16)
    vb_ref[...] = v.astype(BF16)
    lf = _log_sigmoid(z[:, 3 * hd:] + bf_ref[...])
    lf_ref[...] = lf[:, :n_heads]
    hi, lo = _split2(lf)
    c = _dot(tri_ref[...], hi) + _dot(tri_ref[...], lo)
    if tiles_per_seq is not None:
        @pl.when(pl.program_id(0) % tiles_per_seq == 0)
        def _():
            carry_ref[...] = jnp.zeros_like(carry_ref)
        c = c + carry_ref[...]
        carry_ref[...] = c[c.shape[0] - 1:, :]
    c_ref[...] = c[:, :n_heads]


def fox_project(x, g, w_pad, bf_pad, *, seq_len, group):
    n, d = x.shape
    n_heads = d // HEAD_DIM
    hd = n_heads * HEAD_DIM
    tm = min(ROW_TILE, n)
    row = jnp.arange(tm)[:, None]
    col = jnp.arange(tm)[None, :]
    tri = col <= row
    if group is not None:
        tri = tri & ((row // group) == (col // group))
        tiles_per_seq = None
    else:
        tiles_per_seq = seq_len // tm
    tri = tri.astype(BF16)
    body = functools.partial(_fox_proj_body, hd=hd, n_heads=n_heads, tiles_per_seq=tiles_per_seq)
    wide = lambda dt: jax.ShapeDtypeStruct((n, hd), dt)
    thin = jax.ShapeDtypeStruct((n, n_heads), F32)
    return pl.pallas_call(
        body,
        grid=(n // tm,),
        in_specs=[_row_spec(tm, d), _const_spec((1, d)), _const_spec(w_pad.shape),
                  _const_spec(bf_pad.shape), _const_spec(tri.shape)],
        out_specs=[_row_spec(tm, hd)] * 5 + [_row_spec(tm, n_heads)] * 2,
        out_shape=[wide(BF16), wide(F32), wide(F32), wide(BF16), wide(BF16), thin, thin],
        scratch_shapes=[pltpu.VMEM((1, LANES), F32)],
        compiler_params=_params(1),
        name="fox_project",
    )(x, g, w_pad, bf_pad, tri)


def _fox_flash_body(q_ref, k_ref, v_ref, cq_ref, ck_ref, o_ref, *, tq, tk):
    qi = pl.program_id(2)
    q = q_ref[0]
    lane = lax.broadcasted_iota(jnp.int32, (tq, LANES), 1)
    qpos = qi * tq + lax.broadcasted_iota(jnp.int32, (tq, tk), 0)
    kiota = lax.broadcasted_iota(jnp.int32, (tq, tk), 1)
    n_kv = (qi * tq + tq + tk - 1) // tk
    outs = []
    for hh in range(2):
        in_head = (lane < HEAD_DIM) if hh == 0 else (lane >= HEAD_DIM)
        qm = jnp.where(in_head, q, jnp.zeros_like(q))
        cq = cq_ref[0, 0][:, hh:hh + 1]

        def step(kj, carry, qm=qm, cq=cq, hh=hh):
            m, l, acc = carry
            k0 = pl.multiple_of(kj * tk, tk)
            k = k_ref[0, pl.ds(k0, tk), :]
            v = v_ref[0, pl.ds(k0, tk), :]
            ck = ck_ref[0, 0, hh:hh + 1, pl.ds(k0, tk)]
            s = _dot_nt(qm, k) * SCALE + cq - ck
            s = jnp.where(kiota + kj * tk <= qpos, s, NEG_INF)
            m_new = jnp.maximum(m, jnp.max(s, axis=-1, keepdims=True))
            alpha = jnp.exp(m - m_new)
            p = jnp.exp(s - m_new)
            l = alpha * l + jnp.sum(p, axis=-1, keepdims=True)
            acc = alpha * acc + _dot(p.astype(BF16), v)
            return m_new, l, acc

        init = (jnp.full((tq, 1), NEG_INF, F32), jnp.zeros((tq, 1), F32),
                jnp.zeros((tq, LANES), F32))
        m, l, acc = lax.fori_loop(0, n_kv, step, init)
        outs.append(acc / l)
    o_ref[0] = jnp.where(lane < HEAD_DIM, outs[0], outs[1]).astype(o_ref.dtype)


def fox_attend_prompt(q, kb, vb, c, *, batch, seq_len):
    n, hd = q.shape
    n_heads = hd // HEAD_DIM
    n_pairs = n_heads // 2
    tq = min(FLASH_TILE, seq_len)
    q3 = q.reshape(batch, seq_len, hd)
    k3 = kb.reshape(batch, seq_len, hd)
    v3 = vb.reshape(batch, seq_len, hd)
    c4 = c.reshape(batch, seq_len, n_pairs, 2)
    cq = c4.transpose(0, 2, 1, 3)
    ck = c4.transpose(0, 2, 3, 1)
    body = functools.partial(_fox_flash_body, tq=tq, tk=tq)
    o = pl.pallas_call(
        body,
        grid=(batch, n_pairs, seq_len // tq),
        in_specs=[pl.BlockSpec((1, tq, LANES), lambda b, p, i: (b, i, p)),
                  pl.BlockSpec((1, seq_len, LANES), lambda b, p, i: (b, 0, p)),
                  pl.BlockSpec((1, seq_len, LANES), lambda b, p, i: (b, 0, p)),
                  pl.BlockSpec((1, 1, tq, 2), lambda b, p, i: (b, p, i, 0)),
                  pl.BlockSpec((1, 1, 2, seq_len), lambda b, p, i: (b, p, 0, 0))],
        out_specs=pl.BlockSpec((1, tq, LANES), lambda b, p, i: (b, i, p)),
        out_shape=jax.ShapeDtypeStruct((batch, seq_len, hd), BF16),
        compiler_params=_params(3),
        name="fox_attend_prompt",
    )(q3, k3, v3, cq, ck)
    return o.reshape(n, hd)


def _fox_decode_body(pt_ref, qt_ref, *refs, pages, n_heads, t_new):
    k_refs = refs[:pages]
    v_refs = refs[pages:2 * pages]
    lf_refs = refs[2 * pages:3 * pages]
    (cnrow_ref, cnkeys_ref, knew_ref, vnew_ref, upper_ref, expand_ref, o_ref,
     m_ref, l_ref, acc_ref, carry_ref) = refs[3 * pages:]
    del pt_ref
    step = pl.program_id(1)
    n_cols = LANES

    @pl.when(step == 0)
    def _():
        m_ref[...] = jnp.full_like(m_ref, NEG_INF)
        l_ref[...] = jnp.zeros_like(l_ref)
        acc_ref[...] = jnp.zeros_like(acc_ref)
        carry_ref[...] = jnp.zeros_like(carry_ref)

    qt = qt_ref[0]
    cnrow = cnrow_ref[0]

    def update(s_keys, vb):
        s = s_keys.T
        m_old = m_ref[...]
        m_new = jnp.maximum(m_old, jnp.max(s, axis=-1, keepdims=True))
        alpha = jnp.exp(m_old - m_new)
        p = jnp.exp(s - m_new)
        l_ref[...] = alpha * l_ref[...] + jnp.sum(p, axis=-1, keepdims=True)
        acc_ref[...] = alpha * acc_ref[...] + _dot(p.astype(BF16), vb)
        m_ref[...] = m_new

    for j in reversed(range(pages)):
        kb = k_refs[j][0].astype(BF16)
        hi, lo = _split2(lf_refs[j][0])
        hi_cols = _dot(hi, expand_ref[...])
        lo_cols = _dot(lo, expand_ref[...])
        in_page = _dot(upper_ref[...], hi_cols.astype(BF16)) + _dot(upper_ref[...], lo_cols.astype(BF16))
        s = _dot(kb, qt) * SCALE + in_page + carry_ref[...] + cnrow
        carry_ref[...] = carry_ref[...] + jnp.sum(hi_cols + lo_cols, axis=0, keepdims=True)
        update(s, v_refs[j][0].astype(BF16))

    @pl.when(step == pl.num_programs(1) - 1)
    def _():
        pad_rows = PAGE_SIZE - knew_ref.shape[1]
        kn = jnp.concatenate([knew_ref[0], jnp.zeros((pad_rows, knew_ref.shape[2]), BF16)], axis=0)
        vn = jnp.concatenate([vnew_ref[0], jnp.zeros((pad_rows, vnew_ref.shape[2]), BF16)], axis=0)
        cnk = jnp.concatenate([cnkeys_ref[0], jnp.zeros((pad_rows, n_cols), F32)], axis=0)
        key_t = lax.broadcasted_iota(jnp.int32, (PAGE_SIZE, n_cols), 0)
        qry_t = lax.broadcasted_iota(jnp.int32, (PAGE_SIZE, n_cols), 1) // n_heads
        s = _dot(kn, qt) * SCALE + cnrow - cnk
        s = jnp.where(key_t <= qry_t, s, NEG_INF)
        update(s, vn)
        o = (acc_ref[...] / l_ref[...])[:t_new * n_heads]
        r_head = lax.broadcasted_iota(jnp.int32, o.shape, 0) % n_heads
        l_head = lax.broadcasted_iota(jnp.int32, o.shape, 1) // HEAD_DIM
        o = jnp.where(r_head == l_head, o, 0.0)
        o_ref[0] = jnp.sum(o.reshape(t_new, n_heads, o.shape[1]), axis=1).astype(o_ref.dtype)


def fox_attend_decode(q, kb_new, vb_new, c_new, k_pool, v_pool, lf_pool, page_table, *, t_new):
    n, hd = q.shape
    n_heads = hd // HEAD_DIM
    bd = n // t_new
    n_pages = page_table.shape[1]
    pages = min(PAGES_PER_STEP, n_pages)
    n_steps = n_pages // pages
    real_cols = t_new * n_heads
    n_cols = LANES
    col_pad = n_cols - real_cols
    n_pool = k_pool.shape[0]
    new_rows = 16

    q4 = q.reshape(bd, t_new, n_heads, HEAD_DIM)
    eye = jnp.eye(n_heads, dtype=q.dtype)
    qt = (q4.transpose(0, 2, 3, 1)[..., None] * eye[None, :, None, None, :]).reshape(bd, hd, real_cols)
    qt = jnp.pad(qt, ((0, 0), (0, 0), (0, col_pad)))
    c3 = c_new.reshape(bd, t_new, n_heads)
    cnrow = jnp.pad(c3.reshape(bd, 1, real_cols), ((0, 0), (0, 0), (0, col_pad)))
    cnkeys = jnp.pad(jnp.tile(c3, (1, 1, t_new)), ((0, 0), (0, new_rows - t_new), (0, col_pad)))
    knew = jnp.pad(kb_new.reshape(bd, t_new, hd), ((0, 0), (0, new_rows - t_new), (0, 0)))
    vnew = jnp.pad(vb_new.reshape(bd, t_new, hd), ((0, 0), (0, new_rows - t_new), (0, 0)))
    pr = jnp.arange(PAGE_SIZE)
    upper = (pr[None, :] > pr[:, None]).astype(BF16)
    expand = (jnp.arange(n_cols)[None, :] % n_heads == jnp.arange(n_heads)[:, None]).astype(BF16)
    k2 = k_pool.reshape(n_pool, PAGE_SIZE, hd)
    v2 = v_pool.reshape(n_pool, PAGE_SIZE, hd)

    def page_map(j):
        return lambda b, s, pt: (pt[b, n_pages - (s + 1) * pages + j], 0, 0)

    per_b = lambda shape: pl.BlockSpec((1,) + shape, lambda b, s, pt: (b, 0, 0))
    in_specs = [per_b((hd, n_cols))]
    in_specs += [pl.BlockSpec((1, PAGE_SIZE, hd), page_map(j)) for j in range(pages)]
    in_specs += [pl.BlockSpec((1, PAGE_SIZE, hd), page_map(j)) for j in range(pages)]
    in_specs += [pl.BlockSpec((1, PAGE_SIZE, n_heads), page_map(j)) for j in range(pages)]
    in_specs += [per_b((1, n_cols)), per_b((new_rows, n_cols)), per_b((new_rows, hd)),
                 per_b((new_rows, hd)),
                 pl.BlockSpec((PAGE_SIZE, PAGE_SIZE), lambda b, s, pt: (0, 0)),
                 pl.BlockSpec((n_heads, n_cols), lambda b, s, pt: (0, 0))]
    body = functools.partial(_fox_decode_body, pages=pages, n_heads=n_heads, t_new=t_new)
    o = pl.pallas_call(
        body,
        grid_spec=pltpu.PrefetchScalarGridSpec(
            num_scalar_prefetch=1,
            grid=(bd, n_steps),
            in_specs=in_specs,
            out_specs=pl.BlockSpec((1, t_new, hd), lambda b, s, pt: (b, 0, 0)),
            scratch_shapes=[pltpu.VMEM((n_cols, 1), F32), pltpu.VMEM((n_cols, 1), F32),
                            pltpu.VMEM((n_cols, hd), F32), pltpu.VMEM((1, n_cols), F32)],
        ),
        out_shape=jax.ShapeDtypeStruct((bd, t_new, hd), BF16),
        compiler_params=_params(2),
        name="fox_attend_decode",
    )(page_table, qt, *([k2] * pages), *([v2] * pages), *([lf_pool] * pages),
      cnrow, cnkeys, knew, vnew, upper, expand)
    return o.reshape(n, hd)


def _rope_tile(x, cos, sin):
    lane = lax.broadcasted_iota(jnp.int32, x.shape, 1)
    first_half = (lane % HEAD_DIM) < (HEAD_DIM // 2)
    below = pltpu.roll(x, HEAD_DIM // 2, axis=1)
    above = pltpu.roll(x, LANES - HEAD_DIM // 2, axis=1)
    return x * cos + jnp.where(first_half, -above, below) * sin


def _dup_heads(x):
    lane = lax.broadcasted_iota(jnp.int32, x.shape, 1)
    swapped = pltpu.roll(x, HEAD_DIM, axis=1)
    low = lane < HEAD_DIM
    return jnp.where(low, x, swapped), jnp.where(low, swapped, x)


def _swa_proj_body(x_ref, g_ref, w_ref, cos_ref, sin_ref, q_ref, k_ref, v_ref, kd_ref, vd_ref,
                   *, qd, kd):
    x = x_ref[...]
    h = _rms(x, g_ref[...]).astype(BF16)
    z = _dot(h, w_ref[...])
    cos = cos_ref[...]
    sin = sin_ref[...]
    for j in range(qd // LANES):
        cols = slice(j * LANES, (j + 1) * LANES)
        q_ref[:, cols] = _rope_tile(z[:, cols], cos, sin).astype(BF16)
    for j in range(kd // LANES):
        cols = slice(j * LANES, (j + 1) * LANES)
        k = _rope_tile(z[:, qd + j * LANES:qd + (j + 1) * LANES], cos, sin)
        v = z[:, qd + kd + j * LANES:qd + kd + (j + 1) * LANES]
        k_ref[:, cols] = k
        v_ref[:, cols] = v
        ka, kb = _dup_heads(k)
        va, vb = _dup_heads(v)
        kd_ref[:, 2 * j * LANES:(2 * j + 1) * LANES] = ka.astype(BF16)
        kd_ref[:, (2 * j + 1) * LANES:(2 * j + 2) * LANES] = kb.astype(BF16)
        vd_ref[:, 2 * j * LANES:(2 * j + 1) * LANES] = va.astype(BF16)
        vd_ref[:, (2 * j + 1) * LANES:(2 * j + 2) * LANES] = vb.astype(BF16)


def swa_project(x, g, w_in, cos, sin, *, qd, kd):
    n, d = x.shape
    tm = min(ROW_TILE, n)
    body = functools.partial(_swa_proj_body, qd=qd, kd=kd)
    return pl.pallas_call(
        body,
        grid=(n // tm,),
        in_specs=[_row_spec(tm, d), _const_spec((1, d)), _const_spec(w_in.shape),
                  _row_spec(tm, LANES), _row_spec(tm, LANES)],
        out_specs=[_row_spec(tm, qd), _row_spec(tm, kd), _row_spec(tm, kd),
                   _row_spec(tm, 2 * kd), _row_spec(tm, 2 * kd)],
        out_shape=[jax.ShapeDtypeStruct((n, qd), BF16), jax.ShapeDtypeStruct((n, kd), F32),
                   jax.ShapeDtypeStruct((n, kd), F32), jax.ShapeDtypeStruct((n, 2 * kd), BF16),
                   jax.ShapeDtypeStruct((n, 2 * kd), BF16)],
        compiler_params=_params(1),
        name="swa_project",
    )(x, g, w_in, cos, sin)


def _sink_softmax_rows(s, sink):
    m = jnp.maximum(jnp.max(s, axis=-1, keepdims=True), sink)
    e = jnp.exp(s - m)
    return e / (jnp.sum(e, axis=-1, keepdims=True) + jnp.exp(sink - m))


def _swa_prompt_body(sink_ref, q_ref, kp_ref, kc_ref, vp_ref, vc_ref, o_ref, *, kv_heads, group):
    blk = pl.program_id(1)
    w = WINDOW
    lane = lax.broadcasted_iota(jnp.int32, (w, LANES), 1)
    low = lane < HEAD_DIM
    i = lax.broadcasted_iota(jnp.int32, (w, 2 * w), 0)
    j = lax.broadcasted_iota(jnp.int32, (w, 2 * w), 1)
    diff = w + i - j
    valid = (diff >= 0) & (diff < w) & ((blk * w - w + j) >= 0)
    for hk in range(kv_heads):
        kcols = slice(hk * LANES, (hk + 1) * LANES)
        kk = jnp.concatenate([kp_ref[0, :, kcols], kc_ref[0, :, kcols]], axis=0)
        vv = jnp.concatenate([vp_ref[0, :, kcols], vc_ref[0, :, kcols]], axis=0)
        heads = []
        for g in range(group):
            hq = hk * group + g
            qp = q_ref[0, :, (hq // 2) * LANES:(hq // 2 + 1) * LANES]
            in_head = low if hq % 2 == 0 else jnp.logical_not(low)
            qm = jnp.where(in_head, qp, jnp.zeros_like(qp))
            s = _dot_nt(qm, kk) * SCALE
            s = jnp.where(valid, s, NEG_INF)
            p = _sink_softmax_rows(s, sink_ref[hq])
            heads.append(_dot(p.astype(BF16), vv))
        for pr in range(group // 2):
            o_ref[0, :, (hk * group // 2 + pr) * LANES:(hk * group // 2 + pr + 1) * LANES] = (
                jnp.where(low, heads[2 * pr], heads[2 * pr + 1]).astype(o_ref.dtype))


def swa_attend_prompt(q, kdup, vdup, sinks, *, batch, seq_len, kv_heads):
    n, qd = q.shape
    group = qd // HEAD_DIM // kv_heads
    nblk = seq_len // WINDOW
    q3 = q.reshape(batch, seq_len, qd)
    k3 = kdup.reshape(batch, seq_len, kdup.shape[1])
    v3 = vdup.reshape(batch, seq_len, vdup.shape[1])
    kw = k3.shape[2]
    prev = lambda b, i: (b, jnp.maximum(i - 1, 0), 0)
    cur = lambda b, i: (b, i, 0)
    body = functools.partial(_swa_prompt_body, kv_heads=kv_heads, group=group)
    o = pl.pallas_call(
        body,
        grid=(batch, nblk),
        in_specs=[pl.BlockSpec(memory_space=pltpu.SMEM),
                  pl.BlockSpec((1, WINDOW, qd), cur),
                  pl.BlockSpec((1, WINDOW, kw), prev), pl.BlockSpec((1, WINDOW, kw), cur),
                  pl.BlockSpec((1, WINDOW, kw), prev), pl.BlockSpec((1, WINDOW, kw), cur)],
        out_specs=pl.BlockSpec((1, WINDOW, qd), cur),
        out_shape=jax.ShapeDtypeStruct((batch, seq_len, qd), BF16),
        compiler_params=_params(2),
        name="swa_attend_prompt",
    )(sinks, q3, k3, k3, v3, v3)
    return o.reshape(n, qd)


def _swa_decode_body(q_ref, k_ref, v_ref, sink_ref, o_ref, *, t_new, n_keys):
    q = q_ref[...]
    k = k_ref[...]
    v = v_ref[...]
    s = jnp.einsum("gqd,gkd->gqk", q, k, preferred_element_type=F32) * SCALE
    t = lax.broadcasted_iota(jnp.int32, s.shape, 1) % t_new
    j = lax.broadcasted_iota(jnp.int32, s.shape, 2)
    diff = t + WINDOW - j
    valid = (diff >= 0) & (diff < WINDOW) & (j < n_keys)
    s = jnp.where(valid, s, NEG_INF)
    p = _sink_softmax_rows(s, sink_ref[...])
    o_ref[...] = jnp.einsum("gqk,gkd->gqd", p.astype(BF16), v,
                            preferred_element_type=F32).astype(o_ref.dtype)


def swa_attend_decode(q, k_new, v_new, k_state, v_state, sinks, *, t_new, kv_heads):
    n, qd = q.shape
    bd = n // t_new
    q_heads = qd // HEAD_DIM
    group = q_heads // kv_heads
    n_keys = WINDOW + t_new
    keys_pad = WINDOW + 16
    rows = group * t_new
    qg = q.reshape(bd, t_new, kv_heads, group, HEAD_DIM).transpose(0, 2, 3, 1, 4)
    qg = qg.reshape(bd * kv_heads, rows, HEAD_DIM)

    def cat(state, new):
        new4 = new.reshape(bd, t_new, kv_heads, HEAD_DIM)
        full = jnp.concatenate([state, new4], axis=1)
        full = jnp.pad(full, ((0, 0), (0, keys_pad - n_keys), (0, 0), (0, 0)))
        return full.transpose(0, 2, 1, 3).reshape(bd * kv_heads, keys_pad, HEAD_DIM).astype(BF16)

    kc = cat(k_state, k_new)
    vc = cat(v_state, v_new)
    sink_rows = jnp.broadcast_to(sinks.reshape(1, kv_heads, group, 1, 1),
                                 (bd, kv_heads, group, t_new, 1)).reshape(bd * kv_heads, rows, 1)
    gb = min(64, bd * kv_heads)
    spec = lambda r, c: pl.BlockSpec((gb, r, c), lambda i: (i, 0, 0))
    body = functools.partial(_swa_decode_body, t_new=t_new, n_keys=n_keys)
    o = pl.pallas_call(
        body,
        grid=(bd * kv_heads // gb,),
        in_specs=[spec(rows, HEAD_DIM), spec(keys_pad, HEAD_DIM), spec(keys_pad, HEAD_DIM),
                  spec(rows, 1)],
        out_specs=spec(rows, HEAD_DIM),
        out_shape=jax.ShapeDtypeStruct((bd * kv_heads, rows, HEAD_DIM), BF16),
        compiler_params=_params(1),
        name="swa_attend_decode",
    )(qg, kc, vc, sink_rows)
    o = o.reshape(bd, kv_heads, group, t_new, HEAD_DIM).transpose(0, 3, 1, 2, 4)
    return o.reshape(n, qd)


def _rope_tables(pos):
    half = HEAD_DIM // 2
    inv = ROPE_THETA ** (-jnp.arange(half, dtype=F32) * 2.0 / HEAD_DIM)
    ang = pos.astype(F32)[:, None] * inv[None, :]
    reps = LANES // half
    return jnp.tile(jnp.cos(ang), (1, reps)), jnp.tile(jnp.sin(ang), (1, reps))


def kernel(x_prompt, x_sample, cache_fox_k, cache_fox_v, cache_fox_logf, state_swa_k, state_swa_v,
           page_table, norm_g, ffn_w_in, ffn_w_out, sgu_w_in, sgu_ln_g, sgu_ln_b, sgu_w_s, sgu_b_s,
           sgu_w_out, fox_w_in, fox_b_f, fox_w_out, swa_w_in, swa_sinks, swa_w_out):
    batch, seq_len, d = x_prompt.shape
    bd, t_new, _ = x_sample.shape
    depth = norm_g.shape[0]
    past_len = page_table.shape[1] * PAGE_SIZE
    n_heads = d // HEAD_DIM
    kv_heads = state_swa_k.shape[3]
    kd = kv_heads * HEAD_DIM

    xp = x_prompt.reshape(batch * seq_len, d)
    xs = x_sample.reshape(bd * t_new, d)
    ffn_wi = ffn_w_in.astype(BF16)
    ffn_wo = ffn_w_out.astype(BF16)
    cos_p, sin_p = _rope_tables(jnp.arange(seq_len))
    cos_p = jnp.tile(cos_p, (batch, 1))
    sin_p = jnp.tile(sin_p, (batch, 1))
    cos_s, sin_s = _rope_tables(past_len + jnp.arange(t_new))
    cos_s = jnp.tile(cos_s, (bd, 1))
    sin_s = jnp.tile(sin_s, (bd, 1))

    fox_k_p, fox_v_p, fox_lf_p, fox_k_s, fox_v_s, fox_lf_s = [], [], [], [], [], []
    swa_k_p, swa_v_p, swa_k_s, swa_v_s = [], [], [], []
    sgu_v_s = []
    for layer in range(depth):
        g = norm_g[layer].reshape(6, 1, d)
        xp = half_ffn(xp, g[0], g[1], ffn_wi[layer, 0], ffn_wo[layer, 0])
        xs = half_ffn(xs, g[0], g[1], ffn_wi[layer, 0], ffn_wo[layer, 0])
        kind = layer % 3
        j = layer // 3
        if kind == 0:
            d_sgu = sgu_w_out.shape[1]
            w_in = sgu_w_in[j].astype(BF16)
            w_out = sgu_w_out[j].astype(BF16)
            ln_g = sgu_ln_g[j].reshape(1, -1)
            ln_b = sgu_ln_b[j].reshape(1, -1)
            w_s = sgu_w_s[j]
            b_s = sgu_b_s[j]
            reps = CHUNK // t_new
            w_s_dec = jnp.tile(w_s[:, :t_new, :t_new], (1, reps, reps))
            b_s_dec = jnp.tile(b_s[:, :t_new], (1, reps))
            xp, _ = gmlp_mixer(xp, g[2], g[3], w_in, ln_g, ln_b, w_s, b_s.T, w_out, sample=False)
            xs, v_rows = gmlp_mixer(xs, g[2], g[3], w_in, ln_g, ln_b, w_s_dec, b_s_dec.T, w_out,
                                    sample=True)
            sgu_v_s.append(v_rows.reshape(bd, t_new, -1))
        elif kind == 1:
            hd = n_heads * HEAD_DIM
            w = fox_w_in[j]
            w_pad = jnp.pad(w, ((0, 0), (0, LANES - n_heads))).astype(BF16)
            bf_pad = jnp.pad(fox_b_f[j], (0, LANES - n_heads)).reshape(1, LANES)
            w_out = fox_w_out[j].astype(BF16)
            q, k, v, kb, vb, lf, c = fox_project(xp, g[2], w_pad, bf_pad, seq_len=seq_len, group=None)
            o = fox_attend_prompt(q, kb, vb, c, batch=batch, seq_len=seq_len)
            xp = attn_out(xp, o, w_out, g[3])
            fox_k_p.append(k.reshape(batch, seq_len, n_heads, HEAD_DIM))
            fox_v_p.append(v.reshape(batch, seq_len, n_heads, HEAD_DIM))
            fox_lf_p.append(lf.reshape(batch, seq_len, n_heads))
            q, k, v, kb, vb, lf, c = fox_project(xs, g[2], w_pad, bf_pad, seq_len=None, group=t_new)
            o = fox_attend_decode(q, kb, vb, c, cache_fox_k[j], cache_fox_v[j], cache_fox_logf[j],
                                  page_table, t_new=t_new)
            xs = attn_out(xs, o, w_out, g[3])
            fox_k_s.append(k.reshape(bd, t_new, n_heads, HEAD_DIM))
            fox_v_s.append(v.reshape(bd, t_new, n_heads, HEAD_DIM))
            fox_lf_s.append(lf.reshape(bd, t_new, n_heads))
        else:
            qd = n_heads * HEAD_DIM
            w_in = swa_w_in[j].astype(BF16)
            w_out = swa_w_out[j].astype(BF16)
            q, k, v, kdup, vdup = swa_project(xp, g[2], w_in, cos_p, sin_p, qd=qd, kd=kd)
            o = swa_attend_prompt(q, kdup, vdup, swa_sinks[j], batch=batch, seq_len=seq_len,
                                  kv_heads=kv_heads)
            xp = attn_out(xp, o, w_out, g[3])
            swa_k_p.append(k.reshape(batch, seq_len, kv_heads, HEAD_DIM)[:, -WINDOW:])
            swa_v_p.append(v.reshape(batch, seq_len, kv_heads, HEAD_DIM)[:, -WINDOW:])
            q, k, v, _, _ = swa_project(xs, g[2], w_in, cos_s, sin_s, qd=qd, kd=kd)
            o = swa_attend_decode(q, k, v, state_swa_k[j], state_swa_v[j], swa_sinks[j],
                                  t_new=t_new, kv_heads=kv_heads)
            xs = attn_out(xs, o, w_out, g[3])
            k4 = k.reshape(bd, t_new, kv_heads, HEAD_DIM)
            v4 = v.reshape(bd, t_new, kv_heads, HEAD_DIM)
            swa_k_s.append(jnp.concatenate([state_swa_k[j], k4], axis=1)[:, -WINDOW:])
            swa_v_s.append(jnp.concatenate([state_swa_v[j], v4], axis=1)[:, -WINDOW:])
        xp = half_ffn(xp, g[4], g[5], ffn_wi[layer, 1], ffn_wo[layer, 1])
        xs = half_ffn(xs, g[4], g[5], ffn_wi[layer, 1], ffn_wo[layer, 1])
    return (xp.reshape(batch, seq_len, d), xs.reshape(bd, t_new, d),
            jnp.stack(fox_k_p), jnp.stack(fox_v_p), jnp.stack(fox_lf_p),
            jnp.stack(fox_k_s), jnp.stack(fox_v_s), jnp.stack(fox_lf_s),
            jnp.stack(swa_k_p), jnp.stack(swa_v_p), jnp.stack(swa_k_s), jnp.stack(swa_v_s),
            jnp.stack(sgu_v_s))
```

```python
import functools

import jax
import jax.numpy as jnp
from jax import lax
from jax.experimental import pallas as pl
from jax.experimental.pallas import tpu as pltpu

F32 = jnp.float32
BF16 = jnp.bfloat16

HEAD_DIM = 64
CHUNK = 128
SGU_GROUPS = 16
WINDOW = 128
PAGE_SIZE = 128
ROPE_THETA = 10000.0
NORM_EPS = 1e-6
RES_HALF = 0.5
NEG_INF = -1e30
SCALE = HEAD_DIM ** -0.5

LANES = 128
VMEM_LIMIT_BYTES = 56 * 1024 * 1024
ROW_TILE = 512
FFN_ROW_TILE = 512
GMLP_ROW_TILE = 256
FLASH_TILE = 512
FLASH_KEY_TILE = 512
PAGES_PER_STEP = 8


def _params(n_axes):
    return pltpu.CompilerParams(dimension_semantics=("arbitrary",) * n_axes,
                                vmem_limit_bytes=VMEM_LIMIT_BYTES)


def _const_spec(shape):
    zeros = (0,) * len(shape)
    return pl.BlockSpec(shape, lambda *_: zeros, pipeline_mode=pl.Buffered(1))


def _row_spec(tm, width):
    return pl.BlockSpec((tm, width), lambda i: (i, 0))


def _rms(x, g):
    return x * lax.rsqrt(jnp.mean(x * x, axis=-1, keepdims=True) + NORM_EPS) * g


def _dot(a, b):
    return jnp.dot(a, b, preferred_element_type=F32)


def _dot_nt(a, b):
    return lax.dot_general(a, b, (((1,), (1,)), ((), ())), preferred_element_type=F32)


def _split2(x):
    hi = x.astype(BF16)
    lo = (x - hi.astype(F32)).astype(BF16)
    return hi, lo


def _ffn_body(x_ref, gpre_ref, gpost_ref, wi_ref, wo_ref, o_ref, *, d_ff, chunks):
    x = x_ref[...]
    h = _rms(x, gpre_ref[...]).astype(BF16)
    acc = None
    for c0, cw in chunks:
        gate = _dot(h, wi_ref[:, c0:c0 + cw])
        up = _dot(h, wi_ref[:, d_ff + c0:d_ff + c0 + cw])
        act = (gate * jax.nn.sigmoid(gate) * up).astype(BF16)
        y = _dot(act, wo_ref[c0:c0 + cw, :])
        acc = y if acc is None else acc + y
    o_ref[...] = x + RES_HALF * _rms(acc, gpost_ref[...])


def _ff_chunks(d_ff, width=1024):
    out, c0 = [], 0
    while c0 < d_ff:
        cw = min(width, d_ff - c0)
        out.append((c0, cw))
        c0 += cw
    return tuple(out)


def half_ffn(x, g_pre, g_post, w_in, w_out):
    n, d = x.shape
    d_ff = w_out.shape[0]
    tm = min(FFN_ROW_TILE, n)
    body = functools.partial(_ffn_body, d_ff=d_ff, chunks=_ff_chunks(d_ff))
    return pl.pallas_call(
        body,
        grid=(n // tm,),
        in_specs=[_row_spec(tm, d), _const_spec((1, d)), _const_spec((1, d)),
                  _const_spec(w_in.shape), _const_spec(w_out.shape)],
        out_specs=_row_spec(tm, d),
        out_shape=jax.ShapeDtypeStruct((n, d), F32),
        compiler_params=_params(1),
        name="half_ffn",
    )(x, g_pre, g_post, w_in, w_out)


def _gmlp_body(x_ref, g2_ref, g3_ref, win_ref, lng_ref, lnb_ref, ws_ref, bs_ref, wout_ref,
               *out_refs, d_sgu, tm, sample, emit_v):
    if emit_v:
        o_ref, v_ref, gated_ref = out_refs
    else:
        o_ref, gated_ref = out_refs
    x = x_ref[...]
    h = _rms(x, g2_ref[...]).astype(BF16)
    u = jax.nn.gelu(_dot(h, win_ref[:, :d_sgu]), approximate=True)
    zv = jax.nn.gelu(_dot(h, win_ref[:, d_sgu:]), approximate=True)
    mu = jnp.mean(zv, axis=-1, keepdims=True)
    var = jnp.mean(jnp.square(zv - mu), axis=-1, keepdims=True)
    v = (zv - mu) * lax.rsqrt(var + NORM_EPS) * lng_ref[...] + lnb_ref[...]
    if emit_v:
        v_ref[...] = v
    vb = v.astype(BF16)

    row = lax.broadcasted_iota(jnp.int32, (CHUNK, CHUNK), 0)
    col = lax.broadcasted_iota(jnp.int32, (CHUNK, CHUNK), 1)
    mask = col <= row
    if sample:
        mask = mask & ((row // 4) == (col // 4))
    gw = d_sgu // SGU_GROUPS
    for g in range(SGU_GROUPS):
        w = jnp.where(mask, ws_ref[g], 0.0).astype(BF16)
        b = bs_ref[:, g:g + 1]
        for c in range(tm // CHUNK):
            rows = slice(c * CHUNK, (c + 1) * CHUNK)
            cols = slice(g * gw, (g + 1) * gw)
            mix = _dot(w, vb[rows, cols]) + b
            gated_ref[rows, cols] = (u[rows, cols] * mix).astype(BF16)
    y = _dot(gated_ref[...], wout_ref[...])
    o_ref[...] = x + _rms(y, g3_ref[...])


def gmlp_mixer(x, g2, g3, w_in, ln_g, ln_b, w_s, b_s_t, w_out, *, sample):
    n, d = x.shape
    d_sgu = w_out.shape[0]
    tm = min(GMLP_ROW_TILE, n)
    body = functools.partial(_gmlp_body, d_sgu=d_sgu, tm=tm, sample=sample, emit_v=sample)
    out_shape = [jax.ShapeDtypeStruct((n, d), F32)]
    out_specs = [_row_spec(tm, d)]
    if sample:
        out_shape.append(jax.ShapeDtypeStruct((n, d_sgu), F32))
        out_specs.append(_row_spec(tm, d_sgu))
    res = pl.pallas_call(
        body,
        grid=(n // tm,),
        in_specs=[_row_spec(tm, d), _const_spec((1, d)), _const_spec((1, d)),
                  _const_spec(w_in.shape), _const_spec((1, d_sgu)), _const_spec((1, d_sgu)),
                  _const_spec(w_s.shape), _const_spec(b_s_t.shape), _const_spec(w_out.shape)],
        out_specs=out_specs,
        out_shape=out_shape,
        scratch_shapes=[pltpu.VMEM((tm, d_sgu), BF16)],
        compiler_params=_params(1),
        name="gmlp_sample" if sample else "gmlp_prompt",
    )(x, g2, g3, w_in, ln_g, ln_b, w_s, b_s_t, w_out)
    return res if sample else (res[0], None)


def _attn_out_body(x_ref, o_ref, w_ref, g_ref, y_ref):
    y = _dot(o_ref[...], w_ref[...])
    y_ref[...] = x_ref[...] + _rms(y, g_ref[...])


def attn_out(x, o, w_out, g_post):
    n, d = x.shape
    tm = min(ROW_TILE, n)
    return pl.pallas_call(
        _attn_out_body,
        grid=(n // tm,),
        in_specs=[_row_spec(tm, d), _row_spec(tm, o.shape[1]), _const_spec(w_out.shape),
                  _const_spec((1, d))],
        out_specs=_row_spec(tm, d),
        out_shape=jax.ShapeDtypeStruct((n, d), F32),
        compiler_params=_params(1),
        name="attn_out",
    )(x, o, w_out, g_post)


def _attn_out_t_body(x_ref, ot_ref, w_ref, g_ref, y_ref):
    y = lax.dot_general(ot_ref[0], w_ref[...], (((0,), (0,)), ((), ())), preferred_element_type=F32)
    y_ref[...] = x_ref[...] + _rms(y, g_ref[...])


def attn_out_t(x, o_t, w_out, g_post):
    n, d = x.shape
    _, hd, seq_len = o_t.shape
    tm = min(ROW_TILE, seq_len)
    tps = seq_len // tm
    return pl.pallas_call(
        _attn_out_t_body,
        grid=(n // tm,),
        in_specs=[_row_spec(tm, d), pl.BlockSpec((1, hd, tm), lambda i: (i // tps, 0, i % tps)),
                  _const_spec(w_out.shape), _const_spec((1, d))],
        out_specs=_row_spec(tm, d),
        out_shape=jax.ShapeDtypeStruct((n, d), F32),
        compiler_params=_params(1),
        name="attn_out_t",
    )(x, o_t, w_out, g_post)


def _log_sigmoid(x):
    return jnp.minimum(x, 0.0) - jnp.log1p(jnp.exp(-jnp.abs(x)))


def _fox_proj_body(x_ref, g_ref, w_ref, bf_ref, tri_ref, q_ref, k_ref, v_ref, kb_ref, vb_ref,
                   lf_ref, c_ref, *, hd, n_heads):
    x = x_ref[...]
    h = _rms(x, g_ref[...]).astype(BF16)
    z = _dot(h, w_ref[...])
    q_ref[...] = (z[:, :hd] * SCALE).astype(BF16)
    k = z[:, hd:2 * hd]
    v = z[:, 2 * hd:3 * hd]
    k_ref[...] = k
    v_ref[...] = v
    kb_ref[...] = k.astype(BF16)
    vb_ref[...] = v.astype(BF16)
    lf = _log_sigmoid(z[:, 3 * hd:] + bf_ref[...])
    lf_ref[...] = lf[:, :n_heads]
    hi, lo = _split2(lf)
    c = _dot(tri_ref[...], hi) + _dot(tri_ref[...], lo)
    c_ref[...] = c[:, :n_heads]


def fox_project_decode(x, g, w_pad, bf_pad, *, group):
    n, d = x.shape
    n_heads = d // HEAD_DIM
    hd = n_heads * HEAD_DIM
    tm = min(ROW_TILE, n)
    row = jnp.arange(tm)[:, None]
    col = jnp.arange(tm)[None, :]
    tri = ((col <= row) & ((row // group) == (col // group))).astype(BF16)
    body = functools.partial(_fox_proj_body, hd=hd, n_heads=n_heads)
    wide = lambda dt: jax.ShapeDtypeStruct((n, hd), dt)
    thin = jax.ShapeDtypeStruct((n, n_heads), F32)
    return pl.pallas_call(
        body,
        grid=(n // tm,),
        in_specs=[_row_spec(tm, d), _const_spec((1, d)), _const_spec(w_pad.shape),
                  _const_spec(bf_pad.shape), _const_spec(tri.shape)],
        out_specs=[_row_spec(tm, hd)] * 5 + [_row_spec(tm, n_heads)] * 2,
        out_shape=[wide(BF16), wide(F32), wide(F32), wide(BF16), wide(BF16), thin, thin],
        compiler_params=_params(1),
        name="fox_project_decode",
    )(x, g, w_pad, bf_pad, tri)


AUG_CQ_HI = HEAD_DIM
AUG_CQ_LO = HEAD_DIM + 1
AUG_CK_HI = HEAD_DIM + 2
AUG_CK_LO = HEAD_DIM + 3
AUG_ROWS = 16


def _fox_proj_prompt_body(x_ref, g_ref, wk_ref, wqt_ref, wkt_ref, wvt_ref, wf_ref, wft_ref, bfr_ref,
                          bfc_ref, tri_ref, selhi_ref, sello_ref, ones_ref,
                          ktok_ref, kt_ref, vt_ref, vtb_ref, qaug_ref, lft_ref, carry_r_ref, carry_c_ref,
                          *, n_heads, tiles_per_seq, tm):
    @pl.when(pl.program_id(0) % tiles_per_seq == 0)
    def _():
        carry_r_ref[...] = jnp.zeros_like(carry_r_ref)
        carry_c_ref[...] = jnp.zeros_like(carry_c_ref)

    x = x_ref[...]
    h = _rms(x, g_ref[...]).astype(BF16)
    tri = tri_ref[...]
    lf = _log_sigmoid(_dot(h, wf_ref[...]) + bfr_ref[...])
    hi, lo = _split2(lf)
    c = _dot(tri, hi) + _dot(tri, lo) + carry_r_ref[...]
    carry_r_ref[...] = c[tm - 1:, :]
    chi, clo = _split2(c)
    k = _dot(h, wk_ref[...]) - _dot(chi, selhi_ref[...]) - _dot(clo, sello_ref[...]) + ones_ref[...]
    ktok_ref[...] = k.astype(BF16)
    kt_ref[0] = _dot_nt(wkt_ref[...], h)
    vt = _dot_nt(wvt_ref[...], h)
    vt_ref[0] = vt
    vtb_ref[0] = vt.astype(BF16)
    qt = _dot_nt(wqt_ref[...], h)
    lft = _log_sigmoid(_dot_nt(wft_ref[...], h) + bfc_ref[...])
    lft_ref[0] = lft
    hit, lot = _split2(lft)
    ct = _dot_nt(hit, tri) + _dot_nt(lot, tri) + carry_c_ref[...]
    carry_c_ref[...] = ct[:, tm - 1:]
    ct_hi = ct.astype(BF16).astype(F32)
    ct_lo = (ct - ct_hi).astype(BF16).astype(F32)
    r = lax.broadcasted_iota(jnp.int32, (AUG_ROWS, tm), 0) + HEAD_DIM
    ones_rows = jnp.where((r == AUG_CK_HI) | (r == AUG_CK_LO), 1.0, 0.0)
    for hd_i in range(n_heads):
        qaug_ref[0, hd_i, 0:HEAD_DIM, :] = qt[hd_i * HEAD_DIM:(hd_i + 1) * HEAD_DIM, :].astype(BF16)
        aug = jnp.where(r == AUG_CQ_HI, ct_hi[hd_i:hd_i + 1, :],
                        jnp.where(r == AUG_CQ_LO, ct_lo[hd_i:hd_i + 1, :], ones_rows))
        qaug_ref[0, hd_i, HEAD_DIM:HEAD_DIM + AUG_ROWS, :] = aug.astype(BF16)
        qaug_ref[0, hd_i, HEAD_DIM + AUG_ROWS:, :] = jnp.zeros((LANES - HEAD_DIM - AUG_ROWS, tm), BF16)


def fox_project_prompt(x, g, w_in, b_f, *, batch, seq_len):
    n, d = x.shape
    n_heads = d // HEAD_DIM
    hd = n_heads * HEAD_DIM
    tm = min(ROW_TILE, seq_len)
    tps = seq_len // tm
    wk = w_in[:, hd:2 * hd].reshape(d, n_heads, HEAD_DIM)
    wk = jnp.pad(wk, ((0, 0), (0, 0), (0, LANES - HEAD_DIM))).reshape(d, n_heads * LANES).astype(BF16)
    wq_t = (w_in[:, :hd] * SCALE).T.astype(BF16)
    wk_t = w_in[:, hd:2 * hd].T.astype(BF16)
    wv_t = w_in[:, 2 * hd:3 * hd].T.astype(BF16)
    wf = jnp.pad(w_in[:, 3 * hd:], ((0, 0), (0, LANES - n_heads))).astype(BF16)
    wf_t = w_in[:, 3 * hd:].T.astype(BF16)
    bf_row = jnp.pad(b_f, (0, LANES - n_heads)).reshape(1, LANES)
    bf_col = b_f.reshape(n_heads, 1)
    tri = (jnp.arange(tm)[None, :] <= jnp.arange(tm)[:, None]).astype(BF16)
    lane = jnp.arange(n_heads * LANES)
    src = jnp.arange(LANES)[:, None]
    sel_hi = ((lane[None, :] // LANES == src) & (lane[None, :] % LANES == AUG_CK_HI)).astype(BF16)
    sel_lo = ((lane[None, :] // LANES == src) & (lane[None, :] % LANES == AUG_CK_LO)).astype(BF16)
    ones = ((lane % LANES == AUG_CQ_HI) | (lane % LANES == AUG_CQ_LO)).astype(F32).reshape(1, -1)
    body = functools.partial(_fox_proj_prompt_body, n_heads=n_heads, tiles_per_seq=tps, tm=tm)
    consts = [g, wk, wq_t, wk_t, wv_t, wf, wf_t, bf_row, bf_col, tri, sel_hi, sel_lo, ones]
    feat = lambda rows: pl.BlockSpec((1, rows, tm), lambda i: (i // tps, 0, i % tps))
    return pl.pallas_call(
        body,
        grid=(n // tm,),
        in_specs=[_row_spec(tm, d)] + [_const_spec(c.shape) for c in consts],
        out_specs=[_row_spec(tm, n_heads * LANES), feat(hd), feat(hd), feat(hd),
                   pl.BlockSpec((1, n_heads, LANES, tm), lambda i: (i // tps, 0, 0, i % tps)),
                   feat(n_heads)],
        out_shape=[jax.ShapeDtypeStruct((n, n_heads * LANES), BF16),
                   jax.ShapeDtypeStruct((batch, hd, seq_len), F32),
                   jax.ShapeDtypeStruct((batch, hd, seq_len), F32),
                   jax.ShapeDtypeStruct((batch, hd, seq_len), BF16),
                   jax.ShapeDtypeStruct((batch, n_heads, LANES, seq_len), BF16),
                   jax.ShapeDtypeStruct((batch, n_heads, seq_len), F32)],
        scratch_shapes=[pltpu.VMEM((1, LANES), F32), pltpu.VMEM((n_heads, 1), F32)],
        compiler_params=_params(1),
        name="fox_project_prompt",
    )(x, *consts)


def _fox_flash_body(q_ref, k_ref, v_ref, o_ref, m_ref, l_ref, acc_ref, *, tq, tk):
    qi = pl.program_id(2)
    n_diag = tq // tk
    key_i = lax.broadcasted_iota(jnp.int32, (tk, tq), 0)
    qry_i = lax.broadcasted_iota(jnp.int32, (tk, tq), 1)
    m_ref[...] = jnp.full_like(m_ref, NEG_INF)
    l_ref[...] = jnp.zeros_like(l_ref)
    acc_ref[...] = jnp.zeros_like(acc_ref)

    def block(k0, mask):
        for hh in range(2):
            qt = q_ref[0, hh]
            k = k_ref[0, pl.ds(k0, tk), hh * LANES:(hh + 1) * LANES]
            vt = v_ref[0, hh * HEAD_DIM:(hh + 1) * HEAD_DIM, pl.ds(k0, tk)]
            s = _dot(k, qt)
            if mask is not None:
                s = jnp.where(mask, s, NEG_INF)
            m_old = m_ref[hh]
            m_new = jnp.maximum(m_old, jnp.max(s, axis=0, keepdims=True))
            alpha = jnp.exp(m_old - m_new)
            p = jnp.exp(s - m_new)
            l_ref[hh] = alpha * l_ref[hh] + jnp.sum(p, axis=0, keepdims=True)
            acc_ref[hh] = alpha * acc_ref[hh] + _dot(vt, p.astype(BF16))
            m_ref[hh] = m_new

    def full_block(kj, carry):
        block(pl.multiple_of(kj * tk, tk), None)
        return carry

    lax.fori_loop(0, qi * n_diag, full_block, 0)
    for d in range(n_diag):
        block(pl.multiple_of(qi * tq + d * tk, tk), key_i + d * tk <= qry_i)
    for hh in range(2):
        o_ref[0, hh * HEAD_DIM:(hh + 1) * HEAD_DIM, :] = (acc_ref[hh] / l_ref[hh]).astype(o_ref.dtype)


def fox_attend_prompt(q_aug, k_aug, vtb, *, batch, seq_len):
    _, hd, _ = vtb.shape
    n_heads = hd // HEAD_DIM
    n_pairs = n_heads // 2
    tq = min(FLASH_TILE, seq_len)
    tk = min(FLASH_KEY_TILE, tq)
    k3 = k_aug.reshape(batch, seq_len, n_heads * LANES)
    body = functools.partial(_fox_flash_body, tq=tq, tk=tk)
    return pl.pallas_call(
        body,
        grid=(batch, n_pairs, seq_len // tq),
        in_specs=[pl.BlockSpec((1, 2, LANES, tq), lambda b, p, i: (b, p, 0, i)),
                  pl.BlockSpec((1, seq_len, 2 * LANES), lambda b, p, i: (b, 0, p)),
                  pl.BlockSpec((1, 2 * HEAD_DIM, seq_len), lambda b, p, i: (b, p, 0))],
        out_specs=pl.BlockSpec((1, 2 * HEAD_DIM, tq), lambda b, p, i: (b, p, i)),
        out_shape=jax.ShapeDtypeStruct((batch, hd, seq_len), BF16),
        scratch_shapes=[pltpu.VMEM((2, 1, tq), F32), pltpu.VMEM((2, 1, tq), F32),
                        pltpu.VMEM((2, HEAD_DIM, tq), F32)],
        compiler_params=_params(3),
        name="fox_attend_prompt",
    )(q_aug, k3, vtb)


def _fox_decode_body(pt_ref, qbd_ref, cn_ref, *refs, pages, n_heads, t_new):
    k_refs = refs[:pages]
    v_refs = refs[pages:2 * pages]
    lf_refs = refs[2 * pages:3 * pages]
    (knew_ref, vnew_ref, cnk_ref, upper_ref, o_ref, m_ref, l_ref, acc_ref, carry_ref) = refs[3 * pages:]
    del pt_ref
    step = pl.program_id(1)
    reps = LANES // n_heads

    @pl.when(step == 0)
    def _():
        m_ref[...] = jnp.full_like(m_ref, NEG_INF)
        l_ref[...] = jnp.zeros_like(l_ref)
        acc_ref[...] = jnp.zeros_like(acc_ref)
        carry_ref[...] = jnp.zeros_like(carry_ref)

    qbd = qbd_ref[0]
    cn = cn_ref[0]

    def as_row(col):
        return jnp.broadcast_to(col, (LANES, LANES)).T[0:1, :]

    def update(s_blocks, vt_blocks):
        m_old = m_ref[...]
        m_new = m_old
        for s in s_blocks:
            m_new = jnp.maximum(m_new, jnp.max(s, axis=-1, keepdims=True))
        alpha = jnp.exp(m_old - m_new)
        acc = acc_ref[...] * as_row(alpha)
        l = alpha * l_ref[...]
        for s, vt in zip(s_blocks, vt_blocks):
            p = jnp.exp(s - m_new)
            l = l + jnp.sum(p, axis=-1, keepdims=True)
            acc = acc + _dot_nt(vt(), p.astype(BF16))
        acc_ref[...] = acc
        l_ref[...] = l
        m_ref[...] = m_new

    carry = carry_ref[...]
    s_blocks = [None] * pages
    for j in reversed(range(pages)):
        lf = lf_refs[j][0]
        hi, lo = _split2(lf)
        bias = _dot(hi, upper_ref[...]) + _dot(lo, upper_ref[...]) + carry
        bias = jnp.concatenate([bias] * reps, axis=0)
        s_blocks[j] = _dot(qbd, k_refs[j][0].astype(BF16)) + bias + cn
        carry = carry + jnp.sum(lf, axis=-1, keepdims=True)
    carry_ref[...] = carry
    update(s_blocks, [lambda j=j: v_refs[j][0].astype(BF16) for j in range(pages)])

    @pl.when(step == pl.num_programs(1) - 1)
    def _():
        bias = jnp.concatenate([cnk_ref[0]] * reps, axis=0)
        s = _dot(qbd, knew_ref[0]) + bias + cn
        qry_t = lax.broadcasted_iota(jnp.int32, s.shape, 0) // n_heads
        key_t = lax.broadcasted_iota(jnp.int32, s.shape, 1)
        s = jnp.where((key_t <= qry_t) & (key_t < t_new), s, NEG_INF)
        update([s], [lambda: vnew_ref[0]])
        o = (acc_ref[...] / as_row(l_ref[...])).T[:t_new * n_heads]
        r_head = lax.broadcasted_iota(jnp.int32, o.shape, 0) % n_heads
        l_head = lax.broadcasted_iota(jnp.int32, o.shape, 1) // HEAD_DIM
        o = jnp.where(r_head == l_head, o, 0.0)
        o_ref[0] = jnp.sum(o.reshape(t_new, n_heads, o.shape[1]), axis=1).astype(o_ref.dtype)


def fox_attend_decode(q, kb_new, vb_new, c_new, k_pool, v_pool, lf_pool, page_table, *, t_new):
    n, hd = q.shape
    n_heads = hd // HEAD_DIM
    bd = n // t_new
    n_pages = page_table.shape[1]
    pages = min(PAGES_PER_STEP, n_pages)
    n_steps = n_pages // pages
    real_rows = t_new * n_heads
    row_pad = LANES - real_rows
    n_pool = k_pool.shape[0]

    q4 = q.reshape(bd, t_new, n_heads, HEAD_DIM)
    eye = jnp.eye(n_heads, dtype=q.dtype)
    qbd = (q4[:, :, :, None, :] * eye[None, None, :, :, None]).reshape(bd, real_rows, hd)
    qbd = jnp.pad(qbd, ((0, 0), (0, row_pad), (0, 0)))
    c3 = c_new.reshape(bd, t_new, n_heads)
    cn = jnp.pad(c3.reshape(bd, real_rows, 1), ((0, 0), (0, row_pad), (0, 0)))
    key_pad = ((0, 0), (0, 0), (0, PAGE_SIZE - t_new))
    cnk = jnp.pad(-c3.transpose(0, 2, 1), key_pad)
    knew = jnp.pad(kb_new.reshape(bd, t_new, hd).transpose(0, 2, 1), key_pad)
    vnew = jnp.pad(vb_new.reshape(bd, t_new, hd).transpose(0, 2, 1), key_pad)
    pr = jnp.arange(PAGE_SIZE)
    upper = (pr[:, None] > pr[None, :]).astype(BF16)
    kt_pool = k_pool.transpose(0, 2, 3, 1).reshape(n_pool, hd, PAGE_SIZE)
    vt_pool = v_pool.transpose(0, 2, 3, 1).reshape(n_pool, hd, PAGE_SIZE)
    lft_pool = lf_pool.transpose(0, 2, 1)

    def page_map(j):
        return lambda b, s, pt: (pt[b, n_pages - (s + 1) * pages + j], 0, 0)

    per_b = lambda shape: pl.BlockSpec((1,) + shape, lambda b, s, pt: (b, 0, 0))
    in_specs = [per_b((LANES, hd)), per_b((LANES, 1))]
    in_specs += [pl.BlockSpec((1, hd, PAGE_SIZE), page_map(j)) for j in range(pages)]
    in_specs += [pl.BlockSpec((1, hd, PAGE_SIZE), page_map(j)) for j in range(pages)]
    in_specs += [pl.BlockSpec((1, n_heads, PAGE_SIZE), page_map(j)) for j in range(pages)]
    in_specs += [per_b((hd, PAGE_SIZE)), per_b((hd, PAGE_SIZE)), per_b((n_heads, PAGE_SIZE)),
                 pl.BlockSpec((PAGE_SIZE, PAGE_SIZE), lambda b, s, pt: (0, 0))]
    body = functools.partial(_fox_decode_body, pages=pages, n_heads=n_heads, t_new=t_new)
    o = pl.pallas_call(
        body,
        grid_spec=pltpu.PrefetchScalarGridSpec(
            num_scalar_prefetch=1,
            grid=(bd, n_steps),
            in_specs=in_specs,
            out_specs=pl.BlockSpec((1, t_new, hd), lambda b, s, pt: (b, 0, 0)),
            scratch_shapes=[pltpu.VMEM((LANES, 1), F32), pltpu.VMEM((LANES, 1), F32),
                            pltpu.VMEM((hd, LANES), F32), pltpu.VMEM((n_heads, 1), F32)],
        ),
        out_shape=jax.ShapeDtypeStruct((bd, t_new, hd), BF16),
        compiler_params=_params(2),
        name="fox_attend_decode",
    )(page_table, qbd, cn, *([kt_pool] * pages), *([vt_pool] * pages), *([lft_pool] * pages),
      knew, vnew, cnk, upper)
    return o.reshape(n, hd)


def _rope_tile(x, cos, sin):
    lane = lax.broadcasted_iota(jnp.int32, x.shape, 1)
    first_half = (lane % HEAD_DIM) < (HEAD_DIM // 2)
    below = pltpu.roll(x, HEAD_DIM // 2, axis=1)
    above = pltpu.roll(x, LANES - HEAD_DIM // 2, axis=1)
    return x * cos + jnp.where(first_half, -above, below) * sin


def _dup_heads(x):
    lane = lax.broadcasted_iota(jnp.int32, x.shape, 1)
    swapped = pltpu.roll(x, HEAD_DIM, axis=1)
    low = lane < HEAD_DIM
    return jnp.where(low, x, swapped), jnp.where(low, swapped, x)


def _swa_proj_body(x_ref, g_ref, w_ref, cos_ref, sin_ref, q_ref, k_ref, v_ref, kd_ref, vd_ref,
                   *, qd, kd):
    x = x_ref[...]
    h = _rms(x, g_ref[...]).astype(BF16)
    z = _dot(h, w_ref[...])
    cos = cos_ref[...]
    sin = sin_ref[...]
    for j in range(qd // LANES):
        cols = slice(j * LANES, (j + 1) * LANES)
        q_ref[:, cols] = _rope_tile(z[:, cols], cos, sin).astype(BF16)
    for j in range(kd // LANES):
        cols = slice(j * LANES, (j + 1) * LANES)
        k = _rope_tile(z[:, qd + j * LANES:qd + (j + 1) * LANES], cos, sin)
        v = z[:, qd + kd + j * LANES:qd + kd + (j + 1) * LANES]
        k_ref[:, cols] = k
        v_ref[:, cols] = v
        ka, kb = _dup_heads(k)
        va, vb = _dup_heads(v)
        kd_ref[:, 2 * j * LANES:(2 * j + 1) * LANES] = ka.astype(BF16)
        kd_ref[:, (2 * j + 1) * LANES:(2 * j + 2) * LANES] = kb.astype(BF16)
        vd_ref[:, 2 * j * LANES:(2 * j + 1) * LANES] = va.astype(BF16)
        vd_ref[:, (2 * j + 1) * LANES:(2 * j + 2) * LANES] = vb.astype(BF16)


def swa_project(x, g, w_in, cos, sin, *, qd, kd):
    n, d = x.shape
    tm = min(ROW_TILE, n)
    body = functools.partial(_swa_proj_body, qd=qd, kd=kd)
    return pl.pallas_call(
        body,
        grid=(n // tm,),
        in_specs=[_row_spec(tm, d), _const_spec((1, d)), _const_spec(w_in.shape),
                  _row_spec(tm, LANES), _row_spec(tm, LANES)],
        out_specs=[_row_spec(tm, qd), _row_spec(tm, kd), _row_spec(tm, kd),
                   _row_spec(tm, 2 * kd), _row_spec(tm, 2 * kd)],
        out_shape=[jax.ShapeDtypeStruct((n, qd), BF16), jax.ShapeDtypeStruct((n, kd), F32),
                   jax.ShapeDtypeStruct((n, kd), F32), jax.ShapeDtypeStruct((n, 2 * kd), BF16),
                   jax.ShapeDtypeStruct((n, 2 * kd), BF16)],
        compiler_params=_params(1),
        name="swa_project",
    )(x, g, w_in, cos, sin)


def _sink_softmax_rows(s, sink):
    m = jnp.maximum(jnp.max(s, axis=-1, keepdims=True), sink)
    e = jnp.exp(s - m)
    return e / (jnp.sum(e, axis=-1, keepdims=True) + jnp.exp(sink - m))


def _swa_prompt_body(sink_ref, q_ref, kp_ref, kc_ref, vp_ref, vc_ref, o_ref, *, kv_heads, group):
    blk = pl.program_id(1)
    w = WINDOW
    lane = lax.broadcasted_iota(jnp.int32, (w, LANES), 1)
    low = lane < HEAD_DIM
    i = lax.broadcasted_iota(jnp.int32, (w, 2 * w), 0)
    j = lax.broadcasted_iota(jnp.int32, (w, 2 * w), 1)
    diff = w + i - j
    valid = (diff >= 0) & (diff < w) & ((blk * w - w + j) >= 0)
    for hk in range(kv_heads):
        kcols = slice(hk * LANES, (hk + 1) * LANES)
        kk = jnp.concatenate([kp_ref[0, :, kcols], kc_ref[0, :, kcols]], axis=0)
        vv = jnp.concatenate([vp_ref[0, :, kcols], vc_ref[0, :, kcols]], axis=0)
        heads = []
        for g in range(group):
            hq = hk * group + g
            qp = q_ref[0, :, (hq // 2) * LANES:(hq // 2 + 1) * LANES]
            in_head = low if hq % 2 == 0 else jnp.logical_not(low)
            qm = jnp.where(in_head, qp, jnp.zeros_like(qp))
            s = _dot_nt(qm, kk) * SCALE
            s = jnp.where(valid, s, NEG_INF)
            p = _sink_softmax_rows(s, sink_ref[hq])
            heads.append(_dot(p.astype(BF16), vv))
        for pr in range(group // 2):
            o_ref[0, :, (hk * group // 2 + pr) * LANES:(hk * group // 2 + pr + 1) * LANES] = (
                jnp.where(low, heads[2 * pr], heads[2 * pr + 1]).astype(o_ref.dtype))


def swa_attend_prompt(q, kdup, vdup, sinks, *, batch, seq_len, kv_heads):
    n, qd = q.shape
    group = qd // HEAD_DIM // kv_heads
    nblk = seq_len // WINDOW
    q3 = q.reshape(batch, seq_len, qd)
    k3 = kdup.reshape(batch, seq_len, kdup.shape[1])
    v3 = vdup.reshape(batch, seq_len, vdup.shape[1])
    kw = k3.shape[2]
    prev = lambda b, i: (b, jnp.maximum(i - 1, 0), 0)
    cur = lambda b, i: (b, i, 0)
    body = functools.partial(_swa_prompt_body, kv_heads=kv_heads, group=group)
    o = pl.pallas_call(
        body,
        grid=(batch, nblk),
        in_specs=[pl.BlockSpec(memory_space=pltpu.SMEM),
                  pl.BlockSpec((1, WINDOW, qd), cur),
                  pl.BlockSpec((1, WINDOW, kw), prev), pl.BlockSpec((1, WINDOW, kw), cur),
                  pl.BlockSpec((1, WINDOW, kw), prev), pl.BlockSpec((1, WINDOW, kw), cur)],
        out_specs=pl.BlockSpec((1, WINDOW, qd), cur),
        out_shape=jax.ShapeDtypeStruct((batch, seq_len, qd), BF16),
        compiler_params=_params(2),
        name="swa_attend_prompt",
    )(sinks, q3, k3, k3, v3, v3)
    return o.reshape(n, qd)


def _swa_decode_body(q_ref, k_ref, v_ref, sink_ref, o_ref, *, t_new, n_keys):
    q = q_ref[...]
    k = k_ref[...]
    v = v_ref[...]
    s = jnp.einsum("gqd,gkd->gqk", q, k, preferred_element_type=F32) * SCALE
    t = lax.broadcasted_iota(jnp.int32, s.shape, 1) % t_new
    j = lax.broadcasted_iota(jnp.int32, s.shape, 2)
    diff = t + WINDOW - j
    valid = (diff >= 0) & (diff < WINDOW) & (j < n_keys)
    s = jnp.where(valid, s, NEG_INF)
    p = _sink_softmax_rows(s, sink_ref[...])
    o_ref[...] = jnp.einsum("gqk,gkd->gqd", p.astype(BF16), v,
                            preferred_element_type=F32).astype(o_ref.dtype)


def swa_attend_decode(q, k_new, v_new, k_state, v_state, sinks, *, t_new, kv_heads):
    n, qd = q.shape
    bd = n // t_new
    q_heads = qd // HEAD_DIM
    group = q_heads // kv_heads
    n_keys = WINDOW + t_new
    keys_pad = WINDOW + 16
    rows = group * t_new
    qg = q.reshape(bd, t_new, kv_heads, group, HEAD_DIM).transpose(0, 2, 3, 1, 4)
    qg = qg.reshape(bd * kv_heads, rows, HEAD_DIM)

    def cat(state, new):
        new4 = new.reshape(bd, t_new, kv_heads, HEAD_DIM)
        full = jnp.concatenate([state, new4], axis=1)
        full = jnp.pad(full, ((0, 0), (0, keys_pad - n_keys), (0, 0), (0, 0)))
        return full.transpose(0, 2, 1, 3).reshape(bd * kv_heads, keys_pad, HEAD_DIM).astype(BF16)

    kc = cat(k_state, k_new)
    vc = cat(v_state, v_new)
    sink_rows = jnp.broadcast_to(sinks.reshape(1, kv_heads, group, 1, 1),
                                 (bd, kv_heads, group, t_new, 1)).reshape(bd * kv_heads, rows, 1)
    gb = min(64, bd * kv_heads)
    spec = lambda r, c: pl.BlockSpec((gb, r, c), lambda i: (i, 0, 0))
    body = functools.partial(_swa_decode_body, t_new=t_new, n_keys=n_keys)
    o = pl.pallas_call(
        body,
        grid=(bd * kv_heads // gb,),
        in_specs=[spec(rows, HEAD_DIM), spec(keys_pad, HEAD_DIM), spec(keys_pad, HEAD_DIM),
                  spec(rows, 1)],
        out_specs=spec(rows, HEAD_DIM),
        out_shape=jax.ShapeDtypeStruct((bd * kv_heads, rows, HEAD_DIM), BF16),
        compiler_params=_params(1),
        name="swa_attend_decode",
    )(qg, kc, vc, sink_rows)
    o = o.reshape(bd, kv_heads, group, t_new, HEAD_DIM).transpose(0, 3, 1, 2, 4)
    return o.reshape(n, qd)


def _rope_tables(pos):
    half = HEAD_DIM // 2
    inv = ROPE_THETA ** (-jnp.arange(half, dtype=F32) * 2.0 / HEAD_DIM)
    ang = pos.astype(F32)[:, None] * inv[None, :]
    reps = LANES // half
    return jnp.tile(jnp.cos(ang), (1, reps)), jnp.tile(jnp.sin(ang), (1, reps))


def kernel(x_prompt, x_sample, cache_fox_k, cache_fox_v, cache_fox_logf, state_swa_k, state_swa_v,
           page_table, norm_g, ffn_w_in, ffn_w_out, sgu_w_in, sgu_ln_g, sgu_ln_b, sgu_w_s, sgu_b_s,
           sgu_w_out, fox_w_in, fox_b_f, fox_w_out, swa_w_in, swa_sinks, swa_w_out):
    batch, seq_len, d = x_prompt.shape
    bd, t_new, _ = x_sample.shape
    depth = norm_g.shape[0]
    past_len = page_table.shape[1] * PAGE_SIZE
    n_heads = d // HEAD_DIM
    kv_heads = state_swa_k.shape[3]
    kd = kv_heads * HEAD_DIM

    xp = x_prompt.reshape(batch * seq_len, d)
    xs = x_sample.reshape(bd * t_new, d)
    ffn_wi = ffn_w_in.astype(BF16)
    ffn_wo = ffn_w_out.astype(BF16)
    cos_p, sin_p = _rope_tables(jnp.arange(seq_len))
    cos_p = jnp.tile(cos_p, (batch, 1))
    sin_p = jnp.tile(sin_p, (batch, 1))
    cos_s, sin_s = _rope_tables(past_len + jnp.arange(t_new))
    cos_s = jnp.tile(cos_s, (bd, 1))
    sin_s = jnp.tile(sin_s, (bd, 1))

    fox_k_p, fox_v_p, fox_lf_p, fox_k_s, fox_v_s, fox_lf_s = [], [], [], [], [], []
    swa_k_p, swa_v_p, swa_k_s, swa_v_s = [], [], [], []
    sgu_v_s = []
    for layer in range(depth):
        g = norm_g[layer].reshape(6, 1, d)
        xp = half_ffn(xp, g[0], g[1], ffn_wi[layer, 0], ffn_wo[layer, 0])
        xs = half_ffn(xs, g[0], g[1], ffn_wi[layer, 0], ffn_wo[layer, 0])
        kind = layer % 3
        j = layer // 3
        if kind == 0:
            d_sgu = sgu_w_out.shape[1]
            w_in = sgu_w_in[j].astype(BF16)
            w_out = sgu_w_out[j].astype(BF16)
            ln_g = sgu_ln_g[j].reshape(1, -1)
            ln_b = sgu_ln_b[j].reshape(1, -1)
            w_s = sgu_w_s[j]
            b_s = sgu_b_s[j]
            reps = CHUNK // t_new
            w_s_dec = jnp.tile(w_s[:, :t_new, :t_new], (1, reps, reps))
            b_s_dec = jnp.tile(b_s[:, :t_new], (1, reps))
            xp, _ = gmlp_mixer(xp, g[2], g[3], w_in, ln_g, ln_b, w_s, b_s.T, w_out, sample=False)
            xs, v_rows = gmlp_mixer(xs, g[2], g[3], w_in, ln_g, ln_b, w_s_dec, b_s_dec.T, w_out,
                                    sample=True)
            sgu_v_s.append(v_rows.reshape(bd, t_new, -1))
        elif kind == 1:
            hd = n_heads * HEAD_DIM
            w = fox_w_in[j]
            w_pad = jnp.pad(w, ((0, 0), (0, LANES - n_heads))).astype(BF16)
            bf_pad = jnp.pad(fox_b_f[j], (0, LANES - n_heads)).reshape(1, LANES)
            w_out = fox_w_out[j].astype(BF16)
            k_aug, k_t, v_t, vtb, q_aug, lf_t = fox_project_prompt(xp, g[2], w, fox_b_f[j], batch=batch,
                                                                   seq_len=seq_len)
            o_t = fox_attend_prompt(q_aug, k_aug, vtb, batch=batch, seq_len=seq_len)
            xp = attn_out_t(xp, o_t, w_out, g[3])
            to_tokens = lambda a: a.reshape(batch, n_heads, HEAD_DIM, seq_len).transpose(0, 3, 1, 2)
            fox_k_p.append(to_tokens(k_t))
            fox_v_p.append(to_tokens(v_t))
            fox_lf_p.append(lf_t.transpose(0, 2, 1))
            q, k, v, kb, vb, lf, c = fox_project_decode(xs, g[2], w_pad, bf_pad, group=t_new)
            o = fox_attend_decode(q, kb, vb, c, cache_fox_k[j], cache_fox_v[j], cache_fox_logf[j],
                                  page_table, t_new=t_new)
            xs = attn_out(xs, o, w_out, g[3])
            fox_k_s.append(k.reshape(bd, t_new, n_heads, HEAD_DIM))
            fox_v_s.append(v.reshape(bd, t_new, n_heads, HEAD_DIM))
            fox_lf_s.append(lf.reshape(bd, t_new, n_heads))
        else:
            qd = n_heads * HEAD_DIM
            w_in = swa_w_in[j].astype(BF16)
            w_out = swa_w_out[j].astype(BF16)
            q, k, v, kdup, vdup = swa_project(xp, g[2], w_in, cos_p, sin_p, qd=qd, kd=kd)
            o = swa_attend_prompt(q, kdup, vdup, swa_sinks[j], batch=batch, seq_len=seq_len,
                                  kv_heads=kv_heads)
            xp = attn_out(xp, o, w_out, g[3])
            swa_k_p.append(k.reshape(batch, seq_len, kv_heads, HEAD_DIM)[:, -WINDOW:])
            swa_v_p.append(v.reshape(batch, seq_len, kv_heads, HEAD_DIM)[:, -WINDOW:])
            q, k, v, _, _ = swa_project(xs, g[2], w_in, cos_s, sin_s, qd=qd, kd=kd)
            o = swa_attend_decode(q, k, v, state_swa_k[j], state_swa_v[j], swa_sinks[j],
                                  t_new=t_new, kv_heads=kv_heads)
            xs = attn_out(xs, o, w_out, g[3])
            k4 = k.reshape(bd, t_new, kv_heads, HEAD_DIM)
            v4 = v.reshape(bd, t_new, kv_heads, HEAD_DIM)
            swa_k_s.append(jnp.concatenate([state_swa_k[j], k4], axis=1)[:, -WINDOW:])
            swa_v_s.append(jnp.concatenate([state_swa_v[j], v4], axis=1)[:, -WINDOW:])
        xp = half_ffn(xp, g[4], g[5], ffn_wi[layer, 1], ffn_wo[layer, 1])
        xs = half_ffn(xs, g[4], g[5], ffn_wi[layer, 1], ffn_wo[layer, 1])
    return (xp.reshape(batch, seq_len, d), xs.reshape(bd, t_new, d),
            jnp.stack(fox_k_p), jnp.stack(fox_v_p), jnp.stack(fox_lf_p),
            jnp.stack(fox_k_s), jnp.stack(fox_v_s), jnp.stack(fox_lf_s),
            jnp.stack(swa_k_p), jnp.stack(swa_v_p), jnp.stack(swa_k_s), jnp.stack(swa_v_s),
            jnp.stack(sgu_v_s))
```

```python
import functools

import jax
import jax.numpy as jnp
from jax import lax
from jax.experimental import pallas as pl
from jax.experimental.pallas import tpu as pltpu

F32 = jnp.float32
BF16 = jnp.bfloat16

HEAD_DIM = 64
CHUNK = 128
SGU_GROUPS = 16
WINDOW = 128
PAGE_SIZE = 128
ROPE_THETA = 10000.0
NORM_EPS = 1e-6
RES_HALF = 0.5
NEG_INF = -1e30
SCALE = HEAD_DIM ** -0.5

LANES = 128
VMEM_LIMIT_BYTES = 56 * 1024 * 1024
ROW_TILE = 512
FFN_ROW_TILE = 512
GMLP_ROW_TILE = 512
FLASH_TILE = 1024
FLASH_KEY_TILE = 1024
PAGES_PER_STEP = 16


def _params(n_axes):
    return pltpu.CompilerParams(dimension_semantics=("arbitrary",) * n_axes,
                                vmem_limit_bytes=VMEM_LIMIT_BYTES)


def _const_spec(shape):
    zeros = (0,) * len(shape)
    return pl.BlockSpec(shape, lambda *_: zeros, pipeline_mode=pl.Buffered(1))


def _row_spec(tm, width):
    return pl.BlockSpec((tm, width), lambda i: (i, 0))


def _rms(x, g):
    return x * lax.rsqrt(jnp.mean(x * x, axis=-1, keepdims=True) + NORM_EPS) * g


def _dot(a, b):
    return jnp.dot(a, b, preferred_element_type=F32)


def _dot_nt(a, b):
    return lax.dot_general(a, b, (((1,), (1,)), ((), ())), preferred_element_type=F32)


def _split2(x):
    hi = x.astype(BF16)
    lo = (x - hi.astype(F32)).astype(BF16)
    return hi, lo


def _ffn_body(x_ref, gpre_ref, gpost_ref, wi_ref, wo_ref, o_ref, *, d_ff, chunks):
    x = x_ref[...]
    h = _rms(x, gpre_ref[...]).astype(BF16)
    acc = None
    for c0, cw in chunks:
        gate = _dot(h, wi_ref[:, c0:c0 + cw])
        up = _dot(h, wi_ref[:, d_ff + c0:d_ff + c0 + cw])
        act = (gate * jax.nn.sigmoid(gate) * up).astype(BF16)
        y = _dot(act, wo_ref[c0:c0 + cw, :])
        acc = y if acc is None else acc + y
    o_ref[...] = x + RES_HALF * _rms(acc, gpost_ref[...])


def _ff_chunks(d_ff, width=1024):
    out, c0 = [], 0
    while c0 < d_ff:
        cw = min(width, d_ff - c0)
        out.append((c0, cw))
        c0 += cw
    return tuple(out)


def half_ffn(x, g_pre, g_post, w_in_all, w_out_all, layer, half):
    n, d = x.shape
    d_ff = w_out_all.shape[2]
    tm = min(FFN_ROW_TILE, n)
    body = functools.partial(_ffn_body, d_ff=d_ff, chunks=_ff_chunks(d_ff))
    pick = lambda shape: pl.BlockSpec((None, None) + shape, lambda i: (layer, half, 0, 0),
                                      pipeline_mode=pl.Buffered(1))
    return pl.pallas_call(
        body,
        grid=(n // tm,),
        in_specs=[_row_spec(tm, d), _const_spec((1, d)), _const_spec((1, d)),
                  pick(w_in_all.shape[2:]), pick(w_out_all.shape[2:])],
        out_specs=_row_spec(tm, d),
        out_shape=jax.ShapeDtypeStruct((n, d), F32),
        compiler_params=_params(1),
        name="half_ffn",
    )(x, g_pre, g_post, w_in_all, w_out_all)


def _gmlp_body(x_ref, g2_ref, g3_ref, win_ref, lng_ref, lnb_ref, ws_ref, bs_ref, wout_ref,
               *out_refs, d_sgu, tm, sample, emit_v):
    if emit_v:
        o_ref, v_ref, gated_ref = out_refs
    else:
        o_ref, gated_ref = out_refs
    x = x_ref[...]
    h = _rms(x, g2_ref[...]).astype(BF16)
    u = jax.nn.gelu(_dot(h, win_ref[:, :d_sgu]), approximate=True)
    zv = jax.nn.gelu(_dot(h, win_ref[:, d_sgu:]), approximate=True)
    mu = jnp.mean(zv, axis=-1, keepdims=True)
    var = jnp.mean(jnp.square(zv - mu), axis=-1, keepdims=True)
    v = (zv - mu) * lax.rsqrt(var + NORM_EPS) * lng_ref[...] + lnb_ref[...]
    if emit_v:
        v_ref[...] = v
    vb = v.astype(BF16)

    row = lax.broadcasted_iota(jnp.int32, (CHUNK, CHUNK), 0)
    col = lax.broadcasted_iota(jnp.int32, (CHUNK, CHUNK), 1)
    mask = col <= row
    if sample:
        mask = mask & ((row // 4) == (col // 4))
    gw = d_sgu // SGU_GROUPS
    for g in range(SGU_GROUPS):
        w = jnp.where(mask, ws_ref[g], 0.0).astype(BF16)
        b = bs_ref[:, g:g + 1]
        for c in range(tm // CHUNK):
            rows = slice(c * CHUNK, (c + 1) * CHUNK)
            cols = slice(g * gw, (g + 1) * gw)
            mix = _dot(w, vb[rows, cols]) + b
            gated_ref[rows, cols] = (u[rows, cols] * mix).astype(BF16)
    y = _dot(gated_ref[...], wout_ref[...])
    o_ref[...] = x + _rms(y, g3_ref[...])


def gmlp_mixer(x, g2, g3, w_in, ln_g, ln_b, w_s, b_s_t, w_out, *, sample):
    n, d = x.shape
    d_sgu = w_out.shape[0]
    tm = min(GMLP_ROW_TILE, n)
    body = functools.partial(_gmlp_body, d_sgu=d_sgu, tm=tm, sample=sample, emit_v=sample)
    out_shape = [jax.ShapeDtypeStruct((n, d), F32)]
    out_specs = [_row_spec(tm, d)]
    if sample:
        out_shape.append(jax.ShapeDtypeStruct((n, d_sgu), F32))
        out_specs.append(_row_spec(tm, d_sgu))
    res = pl.pallas_call(
        body,
        grid=(n // tm,),
        in_specs=[_row_spec(tm, d), _const_spec((1, d)), _const_spec((1, d)),
                  _const_spec(w_in.shape), _const_spec((1, d_sgu)), _const_spec((1, d_sgu)),
                  _const_spec(w_s.shape), _const_spec(b_s_t.shape), _const_spec(w_out.shape)],
        out_specs=out_specs,
        out_shape=out_shape,
        scratch_shapes=[pltpu.VMEM((tm, d_sgu), BF16)],
        compiler_params=_params(1),
        name="gmlp_sample" if sample else "gmlp_prompt",
    )(x, g2, g3, w_in, ln_g, ln_b, w_s, b_s_t, w_out)
    return res if sample else (res[0], None)


def _attn_out_body(x_ref, o_ref, w_ref, g_ref, y_ref):
    y = _dot(o_ref[...], w_ref[...])
    y_ref[...] = x_ref[...] + _rms(y, g_ref[...])


def attn_out(x, o, w_out, g_post):
    n, d = x.shape
    tm = min(ROW_TILE, n)
    return pl.pallas_call(
        _attn_out_body,
        grid=(n // tm,),
        in_specs=[_row_spec(tm, d), _row_spec(tm, o.shape[1]), _const_spec(w_out.shape),
                  _const_spec((1, d))],
        out_specs=_row_spec(tm, d),
        out_shape=jax.ShapeDtypeStruct((n, d), F32),
        compiler_params=_params(1),
        name="attn_out",
    )(x, o, w_out, g_post)


def _attn_out_t_body(x_ref, ot_ref, w_ref, g_ref, y_ref):
    y = lax.dot_general(ot_ref[0], w_ref[...], (((0,), (0,)), ((), ())), preferred_element_type=F32)
    y_ref[...] = x_ref[...] + _rms(y, g_ref[...])


def attn_out_t(x, o_t, w_out, g_post):
    n, d = x.shape
    _, hd, seq_len = o_t.shape
    tm = min(ROW_TILE, seq_len)
    tps = seq_len // tm
    return pl.pallas_call(
        _attn_out_t_body,
        grid=(n // tm,),
        in_specs=[_row_spec(tm, d), pl.BlockSpec((1, hd, tm), lambda i: (i // tps, 0, i % tps)),
                  _const_spec(w_out.shape), _const_spec((1, d))],
        out_specs=_row_spec(tm, d),
        out_shape=jax.ShapeDtypeStruct((n, d), F32),
        compiler_params=_params(1),
        name="attn_out_t",
    )(x, o_t, w_out, g_post)


def _log_sigmoid(x):
    return jnp.minimum(x, 0.0) - jnp.log1p(jnp.exp(-jnp.abs(x)))


def _fox_proj_body(x_ref, g_ref, w_ref, bf_ref, tri_ref, q_ref, k_ref, v_ref, kb_ref, vb_ref,
                   lf_ref, c_ref, *, hd, n_heads):
    x = x_ref[...]
    h = _rms(x, g_ref[...]).astype(BF16)
    z = _dot(h, w_ref[...])
    q_ref[...] = (z[:, :hd] * SCALE).astype(BF16)
    k = z[:, hd:2 * hd]
    v = z[:, 2 * hd:3 * hd]
    k_ref[...] = k
    v_ref[...] = v
    kb_ref[...] = k.astype(BF16)
    vb_ref[...] = v.astype(BF16)
    lf = _log_sigmoid(z[:, 3 * hd:] + bf_ref[...])
    lf_ref[...] = lf[:, :n_heads]
    hi, lo = _split2(lf)
    c = _dot(tri_ref[...], hi) + _dot(tri_ref[...], lo)
    c_ref[...] = c[:, :n_heads]


def fox_project_decode(x, g, w_pad, bf_pad, *, group):
    n, d = x.shape
    n_heads = d // HEAD_DIM
    hd = n_heads * HEAD_DIM
    tm = min(ROW_TILE, n)
    row = jnp.arange(tm)[:, None]
    col = jnp.arange(tm)[None, :]
    tri = ((col <= row) & ((row // group) == (col // group))).astype(BF16)
    body = functools.partial(_fox_proj_body, hd=hd, n_heads=n_heads)
    wide = lambda dt: jax.ShapeDtypeStruct((n, hd), dt)
    thin = jax.ShapeDtypeStruct((n, n_heads), F32)
    return pl.pallas_call(
        body,
        grid=(n // tm,),
        in_specs=[_row_spec(tm, d), _const_spec((1, d)), _const_spec(w_pad.shape),
                  _const_spec(bf_pad.shape), _const_spec(tri.shape)],
        out_specs=[_row_spec(tm, hd)] * 5 + [_row_spec(tm, n_heads)] * 2,
        out_shape=[wide(BF16), wide(F32), wide(F32), wide(BF16), wide(BF16), thin, thin],
        compiler_params=_params(1),
        name="fox_project_decode",
    )(x, g, w_pad, bf_pad, tri)


AUG_CQ_HI = HEAD_DIM
AUG_CQ_LO = HEAD_DIM + 1
AUG_CK_HI = HEAD_DIM + 2
AUG_CK_LO = HEAD_DIM + 3
AUG_ROWS = 16


def _fox_proj_prompt_body(x_ref, g_ref, wk_ref, wqt_ref, wkt_ref, wvt_ref, wf_ref, wft_ref, bfr_ref,
                          bfc_ref, tri_ref, selhi_ref, sello_ref, ones_ref,
                          ktok_ref, kt_ref, vt_ref, vtb_ref, qaug_ref, lft_ref, carry_r_ref, carry_c_ref,
                          *, n_heads, tiles_per_seq, tm):
    @pl.when(pl.program_id(0) % tiles_per_seq == 0)
    def _():
        carry_r_ref[...] = jnp.zeros_like(carry_r_ref)
        carry_c_ref[...] = jnp.zeros_like(carry_c_ref)

    x = x_ref[...]
    h = _rms(x, g_ref[...]).astype(BF16)
    tri = tri_ref[...]
    lf = _log_sigmoid(_dot(h, wf_ref[...]) + bfr_ref[...])
    hi, lo = _split2(lf)
    c = _dot(tri, hi) + _dot(tri, lo) + carry_r_ref[...]
    carry_r_ref[...] = c[tm - 1:, :]
    chi, clo = _split2(c)
    k = _dot(h, wk_ref[...]) - _dot(chi, selhi_ref[...]) - _dot(clo, sello_ref[...]) + ones_ref[...]
    ktok_ref[...] = k.astype(BF16)
    kt_ref[0] = _dot_nt(wkt_ref[...], h)
    vt = _dot_nt(wvt_ref[...], h)
    vt_ref[0] = vt
    vtb_ref[0] = vt.astype(BF16)
    qt = _dot_nt(wqt_ref[...], h)
    lft = _log_sigmoid(_dot_nt(wft_ref[...], h) + bfc_ref[...])
    lft_ref[0] = lft
    hit, lot = _split2(lft)
    ct = _dot_nt(hit, tri) + _dot_nt(lot, tri) + carry_c_ref[...]
    carry_c_ref[...] = ct[:, tm - 1:]
    ct_hi = ct.astype(BF16).astype(F32)
    ct_lo = (ct - ct_hi).astype(BF16).astype(F32)
    r = lax.broadcasted_iota(jnp.int32, (AUG_ROWS, tm), 0) + HEAD_DIM
    ones_rows = jnp.where((r == AUG_CK_HI) | (r == AUG_CK_LO), 1.0, 0.0)
    for hd_i in range(n_heads):
        qaug_ref[0, hd_i, 0:HEAD_DIM, :] = qt[hd_i * HEAD_DIM:(hd_i + 1) * HEAD_DIM, :].astype(BF16)
        aug = jnp.where(r == AUG_CQ_HI, ct_hi[hd_i:hd_i + 1, :],
                        jnp.where(r == AUG_CQ_LO, ct_lo[hd_i:hd_i + 1, :], ones_rows))
        qaug_ref[0, hd_i, HEAD_DIM:HEAD_DIM + AUG_ROWS, :] = aug.astype(BF16)
        qaug_ref[0, hd_i, HEAD_DIM + AUG_ROWS:, :] = jnp.zeros((LANES - HEAD_DIM - AUG_ROWS, tm), BF16)


def fox_project_prompt(x, g, w_in, b_f, *, batch, seq_len):
    n, d = x.shape
    n_heads = d // HEAD_DIM
    hd = n_heads * HEAD_DIM
    tm = min(ROW_TILE, seq_len)
    tps = seq_len // tm
    wk = w_in[:, hd:2 * hd].reshape(d, n_heads, HEAD_DIM)
    wk = jnp.pad(wk, ((0, 0), (0, 0), (0, LANES - HEAD_DIM))).reshape(d, n_heads * LANES).astype(BF16)
    wq_t = (w_in[:, :hd] * SCALE).T.astype(BF16)
    wk_t = w_in[:, hd:2 * hd].T.astype(BF16)
    wv_t = w_in[:, 2 * hd:3 * hd].T.astype(BF16)
    wf = jnp.pad(w_in[:, 3 * hd:], ((0, 0), (0, LANES - n_heads))).astype(BF16)
    wf_t = w_in[:, 3 * hd:].T.astype(BF16)
    bf_row = jnp.pad(b_f, (0, LANES - n_heads)).reshape(1, LANES)
    bf_col = b_f.reshape(n_heads, 1)
    tri = (jnp.arange(tm)[None, :] <= jnp.arange(tm)[:, None]).astype(BF16)
    lane = jnp.arange(n_heads * LANES)
    src = jnp.arange(LANES)[:, None]
    sel_hi = ((lane[None, :] // LANES == src) & (lane[None, :] % LANES == AUG_CK_HI)).astype(BF16)
    sel_lo = ((lane[None, :] // LANES == src) & (lane[None, :] % LANES == AUG_CK_LO)).astype(BF16)
    ones = ((lane % LANES == AUG_CQ_HI) | (lane % LANES == AUG_CQ_LO)).astype(F32).reshape(1, -1)
    body = functools.partial(_fox_proj_prompt_body, n_heads=n_heads, tiles_per_seq=tps, tm=tm)
    consts = [g, wk, wq_t, wk_t, wv_t, wf, wf_t, bf_row, bf_col, tri, sel_hi, sel_lo, ones]
    feat = lambda rows: pl.BlockSpec((1, rows, tm), lambda i: (i // tps, 0, i % tps))
    return pl.pallas_call(
        body,
        grid=(n // tm,),
        in_specs=[_row_spec(tm, d)] + [_const_spec(c.shape) for c in consts],
        out_specs=[_row_spec(tm, n_heads * LANES), feat(hd), feat(hd), feat(hd),
                   pl.BlockSpec((1, n_heads, LANES, tm), lambda i: (i // tps, 0, 0, i % tps)),
                   feat(n_heads)],
        out_shape=[jax.ShapeDtypeStruct((n, n_heads * LANES), BF16),
                   jax.ShapeDtypeStruct((batch, hd, seq_len), F32),
                   jax.ShapeDtypeStruct((batch, hd, seq_len), F32),
                   jax.ShapeDtypeStruct((batch, hd, seq_len), BF16),
                   jax.ShapeDtypeStruct((batch, n_heads, LANES, seq_len), BF16),
                   jax.ShapeDtypeStruct((batch, n_heads, seq_len), F32)],
        scratch_shapes=[pltpu.VMEM((1, LANES), F32), pltpu.VMEM((n_heads, 1), F32)],
        compiler_params=_params(1),
        name="fox_project_prompt",
    )(x, *consts)


def _fox_flash_body(q_ref, k_ref, v_ref, o_ref, m_ref, l_ref, acc_ref, *, tq, tk):
    qi = pl.program_id(2)
    n_diag = tq // tk
    key_i = lax.broadcasted_iota(jnp.int32, (tk, tq), 0)
    qry_i = lax.broadcasted_iota(jnp.int32, (tk, tq), 1)
    m_ref[...] = jnp.full_like(m_ref, NEG_INF)
    l_ref[...] = jnp.zeros_like(l_ref)
    acc_ref[...] = jnp.zeros_like(acc_ref)

    def block(k0, mask):
        for hh in range(2):
            qt = q_ref[0, hh]
            k = k_ref[0, pl.ds(k0, tk), hh * LANES:(hh + 1) * LANES]
            vt = v_ref[0, hh * HEAD_DIM:(hh + 1) * HEAD_DIM, pl.ds(k0, tk)]
            s = _dot(k, qt)
            if mask is not None:
                s = jnp.where(mask, s, NEG_INF)
            m_old = m_ref[hh]
            m_new = jnp.maximum(m_old, jnp.max(s, axis=0, keepdims=True))
            alpha = jnp.exp(m_old - m_new)
            p = jnp.exp(s - m_new)
            l_ref[hh] = alpha * l_ref[hh] + jnp.sum(p, axis=0, keepdims=True)
            acc_ref[hh] = alpha * acc_ref[hh] + _dot(vt, p.astype(BF16))
            m_ref[hh] = m_new

    def full_block(kj, carry):
        block(pl.multiple_of(kj * tk, tk), None)
        return carry

    lax.fori_loop(0, qi * n_diag, full_block, 0)
    for d in range(n_diag):
        block(pl.multiple_of(qi * tq + d * tk, tk), key_i + d * tk <= qry_i)
    for hh in range(2):
        o_ref[0, hh * HEAD_DIM:(hh + 1) * HEAD_DIM, :] = (acc_ref[hh] / l_ref[hh]).astype(o_ref.dtype)


def fox_attend_prompt(q_aug, k_aug, vtb, *, batch, seq_len):
    _, hd, _ = vtb.shape
    n_heads = hd // HEAD_DIM
    n_pairs = n_heads // 2
    tq = min(FLASH_TILE, seq_len)
    tk = min(FLASH_KEY_TILE, tq)
    k3 = k_aug.reshape(batch, seq_len, n_heads * LANES)
    body = functools.partial(_fox_flash_body, tq=tq, tk=tk)
    return pl.pallas_call(
        body,
        grid=(batch, n_pairs, seq_len // tq),
        in_specs=[pl.BlockSpec((1, 2, LANES, tq), lambda b, p, i: (b, p, 0, i)),
                  pl.BlockSpec((1, seq_len, 2 * LANES), lambda b, p, i: (b, 0, p)),
                  pl.BlockSpec((1, 2 * HEAD_DIM, seq_len), lambda b, p, i: (b, p, 0))],
        out_specs=pl.BlockSpec((1, 2 * HEAD_DIM, tq), lambda b, p, i: (b, p, i)),
        out_shape=jax.ShapeDtypeStruct((batch, hd, seq_len), BF16),
        scratch_shapes=[pltpu.VMEM((2, 1, tq), F32), pltpu.VMEM((2, 1, tq), F32),
                        pltpu.VMEM((2, HEAD_DIM, tq), F32)],
        compiler_params=_params(3),
        name="fox_attend_prompt",
    )(q_aug, k3, vtb)


def _fox_decode_body(pt_ref, qbd_ref, cn_ref, *refs, pages, n_heads, t_new):
    k_refs = refs[:pages]
    v_refs = refs[pages:2 * pages]
    lf_refs = refs[2 * pages:3 * pages]
    (knew_ref, vnew_ref, cnk_ref, upper_ref, o_ref, m_ref, l_ref, acc_ref, carry_ref) = refs[3 * pages:]
    del pt_ref
    step = pl.program_id(1)
    reps = LANES // n_heads

    @pl.when(step == 0)
    def _():
        m_ref[...] = jnp.full_like(m_ref, NEG_INF)
        l_ref[...] = jnp.zeros_like(l_ref)
        acc_ref[...] = jnp.zeros_like(acc_ref)
        carry_ref[...] = jnp.zeros_like(carry_ref)

    qbd = qbd_ref[0]
    cn = cn_ref[0]

    def as_row(col):
        return jnp.broadcast_to(col, (LANES, LANES)).T[0:1, :]

    def update(s_blocks, vt_blocks):
        m_old = m_ref[...]
        m_new = m_old
        for s in s_blocks:
            m_new = jnp.maximum(m_new, jnp.max(s, axis=-1, keepdims=True))
        alpha = jnp.exp(m_old - m_new)
        acc = acc_ref[...] * as_row(alpha)
        l = alpha * l_ref[...]
        for s, vt in zip(s_blocks, vt_blocks):
            p = jnp.exp(s - m_new)
            l = l + jnp.sum(p, axis=-1, keepdims=True)
            acc = acc + _dot_nt(vt(), p.astype(BF16))
        acc_ref[...] = acc
        l_ref[...] = l
        m_ref[...] = m_new

    carry = carry_ref[...]
    s_blocks = [None] * pages
    for j in reversed(range(pages)):
        lf = lf_refs[j][0]
        hi, lo = _split2(lf)
        bias = _dot(hi, upper_ref[...]) + _dot(lo, upper_ref[...]) + carry
        bias = jnp.concatenate([bias] * reps, axis=0)
        s_blocks[j] = _dot(qbd, k_refs[j][0].astype(BF16)) + bias + cn
        carry = carry + jnp.sum(lf, axis=-1, keepdims=True)
    carry_ref[...] = carry
    update(s_blocks, [lambda j=j: v_refs[j][0].astype(BF16) for j in range(pages)])

    @pl.when(step == pl.num_programs(1) - 1)
    def _():
        bias = jnp.concatenate([cnk_ref[0]] * reps, axis=0)
        s = _dot(qbd, knew_ref[0]) + bias + cn
        qry_t = lax.broadcasted_iota(jnp.int32, s.shape, 0) // n_heads
        key_t = lax.broadcasted_iota(jnp.int32, s.shape, 1)
        s = jnp.where((key_t <= qry_t) & (key_t < t_new), s, NEG_INF)
        update([s], [lambda: vnew_ref[0]])
        o = (acc_ref[...] / as_row(l_ref[...])).T[:t_new * n_heads]
        r_head = lax.broadcasted_iota(jnp.int32, o.shape, 0) % n_heads
        l_head = lax.broadcasted_iota(jnp.int32, o.shape, 1) // HEAD_DIM
        o = jnp.where(r_head == l_head, o, 0.0)
        o_ref[0] = jnp.sum(o.reshape(t_new, n_heads, o.shape[1]), axis=1).astype(o_ref.dtype)


def fox_attend_decode(q, kb_new, vb_new, c_new, k_pool, v_pool, lf_pool, page_table, *, t_new):
    n, hd = q.shape
    n_heads = hd // HEAD_DIM
    bd = n // t_new
    n_pages = page_table.shape[1]
    pages = min(PAGES_PER_STEP, n_pages)
    n_steps = n_pages // pages
    real_rows = t_new * n_heads
    row_pad = LANES - real_rows
    n_pool = k_pool.shape[0]

    q4 = q.reshape(bd, t_new, n_heads, HEAD_DIM)
    eye = jnp.eye(n_heads, dtype=q.dtype)
    qbd = (q4[:, :, :, None, :] * eye[None, None, :, :, None]).reshape(bd, real_rows, hd)
    qbd = jnp.pad(qbd, ((0, 0), (0, row_pad), (0, 0)))
    c3 = c_new.reshape(bd, t_new, n_heads)
    cn = jnp.pad(c3.reshape(bd, real_rows, 1), ((0, 0), (0, row_pad), (0, 0)))
    key_pad = ((0, 0), (0, 0), (0, PAGE_SIZE - t_new))
    cnk = jnp.pad(-c3.transpose(0, 2, 1), key_pad)
    knew = jnp.pad(kb_new.reshape(bd, t_new, hd).transpose(0, 2, 1), key_pad)
    vnew = jnp.pad(vb_new.reshape(bd, t_new, hd).transpose(0, 2, 1), key_pad)
    pr = jnp.arange(PAGE_SIZE)
    upper = (pr[:, None] > pr[None, :]).astype(BF16)
    kt_pool = k_pool.transpose(0, 2, 3, 1).reshape(n_pool, hd, PAGE_SIZE)
    vt_pool = v_pool.transpose(0, 2, 3, 1).reshape(n_pool, hd, PAGE_SIZE)
    lft_pool = lf_pool.transpose(0, 2, 1)

    def page_map(j):
        return lambda b, s, pt: (pt[b, n_pages - (s + 1) * pages + j], 0, 0)

    per_b = lambda shape: pl.BlockSpec((1,) + shape, lambda b, s, pt: (b, 0, 0))
    in_specs = [per_b((LANES, hd)), per_b((LANES, 1))]
    in_specs += [pl.BlockSpec((1, hd, PAGE_SIZE), page_map(j)) for j in range(pages)]
    in_specs += [pl.BlockSpec((1, hd, PAGE_SIZE), page_map(j)) for j in range(pages)]
    in_specs += [pl.BlockSpec((1, n_heads, PAGE_SIZE), page_map(j)) for j in range(pages)]
    in_specs += [per_b((hd, PAGE_SIZE)), per_b((hd, PAGE_SIZE)), per_b((n_heads, PAGE_SIZE)),
                 pl.BlockSpec((PAGE_SIZE, PAGE_SIZE), lambda b, s, pt: (0, 0))]
    body = functools.partial(_fox_decode_body, pages=pages, n_heads=n_heads, t_new=t_new)
    o = pl.pallas_call(
        body,
        grid_spec=pltpu.PrefetchScalarGridSpec(
            num_scalar_prefetch=1,
            grid=(bd, n_steps),
            in_specs=in_specs,
            out_specs=pl.BlockSpec((1, t_new, hd), lambda b, s, pt: (b, 0, 0)),
            scratch_shapes=[pltpu.VMEM((LANES, 1), F32), pltpu.VMEM((LANES, 1), F32),
                            pltpu.VMEM((hd, LANES), F32), pltpu.VMEM((n_heads, 1), F32)],
        ),
        out_shape=jax.ShapeDtypeStruct((bd, t_new, hd), BF16),
        compiler_params=_params(2),
        name="fox_attend_decode",
    )(page_table, qbd, cn, *([kt_pool] * pages), *([vt_pool] * pages), *([lft_pool] * pages),
      knew, vnew, cnk, upper)
    return o.reshape(n, hd)


def _rope_tile(x, cos, sin):
    lane = lax.broadcasted_iota(jnp.int32, x.shape, 1)
    first_half = (lane % HEAD_DIM) < (HEAD_DIM // 2)
    below = pltpu.roll(x, HEAD_DIM // 2, axis=1)
    above = pltpu.roll(x, LANES - HEAD_DIM // 2, axis=1)
    return x * cos + jnp.where(first_half, -above, below) * sin


def _dup_heads(x):
    lane = lax.broadcasted_iota(jnp.int32, x.shape, 1)
    swapped = pltpu.roll(x, HEAD_DIM, axis=1)
    low = lane < HEAD_DIM
    return jnp.where(low, x, swapped), jnp.where(low, swapped, x)


def _swa_proj_body(x_ref, g_ref, w_ref, cos_ref, sin_ref, q_ref, k_ref, v_ref, kd_ref, vd_ref,
                   *, qd, kd):
    x = x_ref[...]
    h = _rms(x, g_ref[...]).astype(BF16)
    z = _dot(h, w_ref[...])
    cos = cos_ref[...]
    sin = sin_ref[...]
    for j in range(qd // LANES):
        cols = slice(j * LANES, (j + 1) * LANES)
        q_ref[:, cols] = _rope_tile(z[:, cols], cos, sin).astype(BF16)
    for j in range(kd // LANES):
        cols = slice(j * LANES, (j + 1) * LANES)
        k = _rope_tile(z[:, qd + j * LANES:qd + (j + 1) * LANES], cos, sin)
        v = z[:, qd + kd + j * LANES:qd + kd + (j + 1) * LANES]
        k_ref[:, cols] = k
        v_ref[:, cols] = v
        ka, kb = _dup_heads(k)
        va, vb = _dup_heads(v)
        kd_ref[:, 2 * j * LANES:(2 * j + 1) * LANES] = ka.astype(BF16)
        kd_ref[:, (2 * j + 1) * LANES:(2 * j + 2) * LANES] = kb.astype(BF16)
        vd_ref[:, 2 * j * LANES:(2 * j + 1) * LANES] = va.astype(BF16)
        vd_ref[:, (2 * j + 1) * LANES:(2 * j + 2) * LANES] = vb.astype(BF16)


def swa_project(x, g, w_in, cos, sin, *, qd, kd):
    n, d = x.shape
    tm = min(ROW_TILE, n)
    body = functools.partial(_swa_proj_body, qd=qd, kd=kd)
    return pl.pallas_call(
        body,
        grid=(n // tm,),
        in_specs=[_row_spec(tm, d), _const_spec((1, d)), _const_spec(w_in.shape),
                  _row_spec(tm, LANES), _row_spec(tm, LANES)],
        out_specs=[_row_spec(tm, qd), _row_spec(tm, kd), _row_spec(tm, kd),
                   _row_spec(tm, 2 * kd), _row_spec(tm, 2 * kd)],
        out_shape=[jax.ShapeDtypeStruct((n, qd), BF16), jax.ShapeDtypeStruct((n, kd), F32),
                   jax.ShapeDtypeStruct((n, kd), F32), jax.ShapeDtypeStruct((n, 2 * kd), BF16),
                   jax.ShapeDtypeStruct((n, 2 * kd), BF16)],
        compiler_params=_params(1),
        name="swa_project",
    )(x, g, w_in, cos, sin)


def _sink_softmax_rows(s, sink):
    m = jnp.maximum(jnp.max(s, axis=-1, keepdims=True), sink)
    e = jnp.exp(s - m)
    return e / (jnp.sum(e, axis=-1, keepdims=True) + jnp.exp(sink - m))


def _swa_prompt_body(sink_ref, q_ref, kp_ref, kc_ref, vp_ref, vc_ref, o_ref, *, kv_heads, group):
    blk = pl.program_id(1)
    w = WINDOW
    lane = lax.broadcasted_iota(jnp.int32, (w, LANES), 1)
    low = lane < HEAD_DIM
    i = lax.broadcasted_iota(jnp.int32, (w, 2 * w), 0)
    j = lax.broadcasted_iota(jnp.int32, (w, 2 * w), 1)
    diff = w + i - j
    valid = (diff >= 0) & (diff < w) & ((blk * w - w + j) >= 0)
    for hk in range(kv_heads):
        kcols = slice(hk * LANES, (hk + 1) * LANES)
        kk = jnp.concatenate([kp_ref[0, :, kcols], kc_ref[0, :, kcols]], axis=0)
        vv = jnp.concatenate([vp_ref[0, :, kcols], vc_ref[0, :, kcols]], axis=0)
        heads = []
        for g in range(group):
            hq = hk * group + g
            qp = q_ref[0, :, (hq // 2) * LANES:(hq // 2 + 1) * LANES]
            in_head = low if hq % 2 == 0 else jnp.logical_not(low)
            qm = jnp.where(in_head, qp, jnp.zeros_like(qp))
            s = _dot_nt(qm, kk) * SCALE
            s = jnp.where(valid, s, NEG_INF)
            p = _sink_softmax_rows(s, sink_ref[hq])
            heads.append(_dot(p.astype(BF16), vv))
        for pr in range(group // 2):
            o_ref[0, :, (hk * group // 2 + pr) * LANES:(hk * group // 2 + pr + 1) * LANES] = (
                jnp.where(low, heads[2 * pr], heads[2 * pr + 1]).astype(o_ref.dtype))


def swa_attend_prompt(q, kdup, vdup, sinks, *, batch, seq_len, kv_heads):
    n, qd = q.shape
    group = qd // HEAD_DIM // kv_heads
    nblk = seq_len // WINDOW
    q3 = q.reshape(batch, seq_len, qd)
    k3 = kdup.reshape(batch, seq_len, kdup.shape[1])
    v3 = vdup.reshape(batch, seq_len, vdup.shape[1])
    kw = k3.shape[2]
    prev = lambda b, i: (b, jnp.maximum(i - 1, 0), 0)
    cur = lambda b, i: (b, i, 0)
    body = functools.partial(_swa_prompt_body, kv_heads=kv_heads, group=group)
    o = pl.pallas_call(
        body,
        grid=(batch, nblk),
        in_specs=[pl.BlockSpec(memory_space=pltpu.SMEM),
                  pl.BlockSpec((1, WINDOW, qd), cur),
                  pl.BlockSpec((1, WINDOW, kw), prev), pl.BlockSpec((1, WINDOW, kw), cur),
                  pl.BlockSpec((1, WINDOW, kw), prev), pl.BlockSpec((1, WINDOW, kw), cur)],
        out_specs=pl.BlockSpec((1, WINDOW, qd), cur),
        out_shape=jax.ShapeDtypeStruct((batch, seq_len, qd), BF16),
        compiler_params=_params(2),
        name="swa_attend_prompt",
    )(sinks, q3, k3, k3, v3, v3)
    return o.reshape(n, qd)


def _swa_decode_body(q_ref, k_ref, v_ref, sink_ref, o_ref, *, t_new, n_keys):
    q = q_ref[...]
    k = k_ref[...]
    v = v_ref[...]
    s = jnp.einsum("gqd,gkd->gqk", q, k, preferred_element_type=F32) * SCALE
    t = lax.broadcasted_iota(jnp.int32, s.shape, 1) % t_new
    j = lax.broadcasted_iota(jnp.int32, s.shape, 2)
    diff = t + WINDOW - j
    valid = (diff >= 0) & (diff < WINDOW) & (j < n_keys)
    s = jnp.where(valid, s, NEG_INF)
    p = _sink_softmax_rows(s, sink_ref[...])
    o_ref[...] = jnp.einsum("gqk,gkd->gqd", p.astype(BF16), v,
                            preferred_element_type=F32).astype(o_ref.dtype)


def swa_attend_decode(q, k_new, v_new, k_state, v_state, sinks, *, t_new, kv_heads):
    n, qd = q.shape
    bd = n // t_new
    q_heads = qd // HEAD_DIM
    group = q_heads // kv_heads
    n_keys = WINDOW + t_new
    keys_pad = WINDOW + 16
    rows = group * t_new
    qg = q.reshape(bd, t_new, kv_heads, group, HEAD_DIM).transpose(0, 2, 3, 1, 4)
    qg = qg.reshape(bd * kv_heads, rows, HEAD_DIM)

    def cat(state, new):
        new4 = new.reshape(bd, t_new, kv_heads, HEAD_DIM)
        full = jnp.concatenate([state, new4], axis=1)
        full = jnp.pad(full, ((0, 0), (0, keys_pad - n_keys), (0, 0), (0, 0)))
        return full.transpose(0, 2, 1, 3).reshape(bd * kv_heads, keys_pad, HEAD_DIM).astype(BF16)

    kc = cat(k_state, k_new)
    vc = cat(v_state, v_new)
    sink_rows = jnp.broadcast_to(sinks.reshape(1, kv_heads, group, 1, 1),
                                 (bd, kv_heads, group, t_new, 1)).reshape(bd * kv_heads, rows, 1)
    gb = min(64, bd * kv_heads)
    spec = lambda r, c: pl.BlockSpec((gb, r, c), lambda i: (i, 0, 0))
    body = functools.partial(_swa_decode_body, t_new=t_new, n_keys=n_keys)
    o = pl.pallas_call(
        body,
        grid=(bd * kv_heads // gb,),
        in_specs=[spec(rows, HEAD_DIM), spec(keys_pad, HEAD_DIM), spec(keys_pad, HEAD_DIM),
                  spec(rows, 1)],
        out_specs=spec(rows, HEAD_DIM),
        out_shape=jax.ShapeDtypeStruct((bd * kv_heads, rows, HEAD_DIM), BF16),
        compiler_params=_params(1),
        name="swa_attend_decode",
    )(qg, kc, vc, sink_rows)
    o = o.reshape(bd, kv_heads, group, t_new, HEAD_DIM).transpose(0, 3, 1, 2, 4)
    return o.reshape(n, qd)


def _rope_tables(pos):
    half = HEAD_DIM // 2
    inv = ROPE_THETA ** (-jnp.arange(half, dtype=F32) * 2.0 / HEAD_DIM)
    ang = pos.astype(F32)[:, None] * inv[None, :]
    reps = LANES // half
    return jnp.tile(jnp.cos(ang), (1, reps)), jnp.tile(jnp.sin(ang), (1, reps))


def kernel(x_prompt, x_sample, cache_fox_k, cache_fox_v, cache_fox_logf, state_swa_k, state_swa_v,
           page_table, norm_g, ffn_w_in, ffn_w_out, sgu_w_in, sgu_ln_g, sgu_ln_b, sgu_w_s, sgu_b_s,
           sgu_w_out, fox_w_in, fox_b_f, fox_w_out, swa_w_in, swa_sinks, swa_w_out):
    batch, seq_len, d = x_prompt.shape
    bd, t_new, _ = x_sample.shape
    depth = norm_g.shape[0]
    past_len = page_table.shape[1] * PAGE_SIZE
    n_heads = d // HEAD_DIM
    kv_heads = state_swa_k.shape[3]
    kd = kv_heads * HEAD_DIM

    xp = x_prompt.reshape(batch * seq_len, d)
    xs = x_sample.reshape(bd * t_new, d)
    ffn_wi = ffn_w_in.astype(BF16)
    ffn_wo = ffn_w_out.astype(BF16)
    cos_p, sin_p = _rope_tables(jnp.arange(seq_len))
    cos_p = jnp.tile(cos_p, (batch, 1))
    sin_p = jnp.tile(sin_p, (batch, 1))
    cos_s, sin_s = _rope_tables(past_len + jnp.arange(t_new))
    cos_s = jnp.tile(cos_s, (bd, 1))
    sin_s = jnp.tile(sin_s, (bd, 1))

    fox_k_p, fox_v_p, fox_lf_p, fox_k_s, fox_v_s, fox_lf_s = [], [], [], [], [], []
    swa_k_p, swa_v_p, swa_k_s, swa_v_s = [], [], [], []
    sgu_v_s = []
    for layer in range(depth):
        g = norm_g[layer].reshape(6, 1, d)
        xp = half_ffn(xp, g[0], g[1], ffn_wi, ffn_wo, layer, 0)
        xs = half_ffn(xs, g[0], g[1], ffn_wi, ffn_wo, layer, 0)
        kind = layer % 3
        j = layer // 3
        if kind == 0:
            d_sgu = sgu_w_out.shape[1]
            w_in = sgu_w_in[j].astype(BF16)
            w_out = sgu_w_out[j].astype(BF16)
            ln_g = sgu_ln_g[j].reshape(1, -1)
            ln_b = sgu_ln_b[j].reshape(1, -1)
            w_s = sgu_w_s[j]
            b_s = sgu_b_s[j]
            reps = CHUNK // t_new
            w_s_dec = jnp.tile(w_s[:, :t_new, :t_new], (1, reps, reps))
            b_s_dec = jnp.tile(b_s[:, :t_new], (1, reps))
            xp, _ = gmlp_mixer(xp, g[2], g[3], w_in, ln_g, ln_b, w_s, b_s.T, w_out, sample=False)
            xs, v_rows = gmlp_mixer(xs, g[2], g[3], w_in, ln_g, ln_b, w_s_dec, b_s_dec.T, w_out,
                                    sample=True)
            sgu_v_s.append(v_rows.reshape(bd, t_new, -1))
        elif kind == 1:
            hd = n_heads * HEAD_DIM
            w = fox_w_in[j]
            w_pad = jnp.pad(w, ((0, 0), (0, LANES - n_heads))).astype(BF16)
            bf_pad = jnp.pad(fox_b_f[j], (0, LANES - n_heads)).reshape(1, LANES)
            w_out = fox_w_out[j].astype(BF16)
            k_aug, k_t, v_t, vtb, q_aug, lf_t = fox_project_prompt(xp, g[2], w, fox_b_f[j], batch=batch,
                                                                   seq_len=seq_len)
            o_t = fox_attend_prompt(q_aug, k_aug, vtb, batch=batch, seq_len=seq_len)
            xp = attn_out_t(xp, o_t, w_out, g[3])
            to_tokens = lambda a: a.reshape(batch, n_heads, HEAD_DIM, seq_len).transpose(0, 3, 1, 2)
            fox_k_p.append(to_tokens(k_t))
            fox_v_p.append(to_tokens(v_t))
            fox_lf_p.append(lf_t.transpose(0, 2, 1))
            q, k, v, kb, vb, lf, c = fox_project_decode(xs, g[2], w_pad, bf_pad, group=t_new)
            o = fox_attend_decode(q, kb, vb, c, cache_fox_k[j], cache_fox_v[j], cache_fox_logf[j],
                                  page_table, t_new=t_new)
            xs = attn_out(xs, o, w_out, g[3])
            fox_k_s.append(k.reshape(bd, t_new, n_heads, HEAD_DIM))
            fox_v_s.append(v.reshape(bd, t_new, n_heads, HEAD_DIM))
            fox_lf_s.append(lf.reshape(bd, t_new, n_heads))
        else:
            qd = n_heads * HEAD_DIM
            w_in = swa_w_in[j].astype(BF16)
            w_out = swa_w_out[j].astype(BF16)
            q, k, v, kdup, vdup = swa_project(xp, g[2], w_in, cos_p, sin_p, qd=qd, kd=kd)
            o = swa_attend_prompt(q, kdup, vdup, swa_sinks[j], batch=batch, seq_len=seq_len,
                                  kv_heads=kv_heads)
            xp = attn_out(xp, o, w_out, g[3])
            swa_k_p.append(k.reshape(batch, seq_len, kv_heads, HEAD_DIM)[:, -WINDOW:])
            swa_v_p.append(v.reshape(batch, seq_len, kv_heads, HEAD_DIM)[:, -WINDOW:])
            q, k, v, _, _ = swa_project(xs, g[2], w_in, cos_s, sin_s, qd=qd, kd=kd)
            o = swa_attend_decode(q, k, v, state_swa_k[j], state_swa_v[j], swa_sinks[j],
                                  t_new=t_new, kv_heads=kv_heads)
            xs = attn_out(xs, o, w_out, g[3])
            k4 = k.reshape(bd, t_new, kv_heads, HEAD_DIM)
            v4 = v.reshape(bd, t_new, kv_heads, HEAD_DIM)
            swa_k_s.append(jnp.concatenate([state_swa_k[j], k4], axis=1)[:, -WINDOW:])
            swa_v_s.append(jnp.concatenate([state_swa_v[j], v4], axis=1)[:, -WINDOW:])
        xp = half_ffn(xp, g[4], g[5], ffn_wi, ffn_wo, layer, 1)
        xs = half_ffn(xs, g[4], g[5], ffn_wi, ffn_wo, layer, 1)
    return (xp.reshape(batch, seq_len, d), xs.reshape(bd, t_new, d),
            jnp.stack(fox_k_p), jnp.stack(fox_v_p), jnp.stack(fox_lf_p),
            jnp.stack(fox_k_s), jnp.stack(fox_v_s), jnp.stack(fox_lf_s),
            jnp.stack(swa_k_p), jnp.stack(swa_v_p), jnp.stack(swa_k_s), jnp.stack(swa_v_s),
            jnp.stack(sgu_v_s))
```
